```python
import math
import jax, jax.numpy as jnp
from jax import lax
import numpy as np

D_MODEL = 1024
BATCH = 4
SEQ = 4096
DEPTH = 4
DEC_BATCH = 32
DEC_SEQ = 8
PAST_LEN = 8192
PAGE_SIZE = 128

N_A_LAYERS = DEPTH // 2
N_B_LAYERS = DEPTH - N_A_LAYERS
SSM_WIDTH = D_MODEL
SSM_GROUP = 16
N_GROUPS = SSM_WIDTH // SSM_GROUP
SSM_STATE = 64
N_HEADS = 16
HEAD_DIM = 64
ATT_WIDTH = N_HEADS * HEAD_DIM
Q_BLOCK = 128
RMS_EPS = 1e-6
DT_MIN = 0.001
DT_MAX = 0.1
SB_BIAS_INIT = -12.0

kernel_name = 'yoco_s5_stickbreaking_step'


def _rms_norm(x, g):
    x32 = x.astype(jnp.float32)
    y = x32 * lax.rsqrt(jnp.mean(x32 * x32, axis=-1, keepdims=True) + RMS_EPS)
    return (y * g.astype(jnp.float32)).astype(x.dtype)


def _ssm_combine(e1, e2):
    a1r, a1i, b1r, b1i = e1
    a2r, a2i, b2r, b2i = e2
    return (a2r * a1r - a2i * a1i,
            a2r * a1i + a2i * a1r,
            a2r * b1r - a2i * b1i + b2r,
            a2r * b1i + a2i * b1r + b2i)


def _s5_scan(u, h0_re, h0_im, log_dt, a_re, a_im, b_re, b_im, c_re, c_im, d_skip):
    f32 = jnp.float32
    n, t, _ = u.shape
    u32 = u.astype(f32).reshape(n, t, N_GROUPS, SSM_GROUP)
    a_re = a_re.astype(f32)
    a_im = a_im.astype(f32)
    dt = jnp.exp(log_dt.astype(f32))[:, None]
    mag = jnp.exp(dt * a_re)
    lam_re = mag * jnp.cos(dt * a_im)
    lam_im = mag * jnp.sin(dt * a_im)
    den = a_re * a_re + a_im * a_im
    w_re = ((lam_re - 1.0) * a_re + lam_im * a_im) / den
    w_im = (lam_im * a_re - (lam_re - 1.0) * a_im) / den
    r = jnp.einsum('ntgc,gpc->ntgp', u32, b_re.astype(f32))
    i = jnp.einsum('ntgc,gpc->ntgp', u32, b_im.astype(f32))
    bu_re = w_re * r - w_im * i
    bu_im = w_re * i + w_im * r
    if h0_re is not None:
        h0_re = h0_re.astype(f32)
        h0_im = h0_im.astype(f32)
        bu_re = bu_re.at[:, 0].add(lam_re * h0_re - lam_im * h0_im)
        bu_im = bu_im.at[:, 0].add(lam_re * h0_im + lam_im * h0_re)
    lr = jnp.broadcast_to(lam_re, bu_re.shape)
    li = jnp.broadcast_to(lam_im, bu_im.shape)
    _, _, h_re, h_im = lax.associative_scan(_ssm_combine, (lr, li, bu_re, bu_im), axis=1)
    y = (jnp.einsum('ntgp,gcp->ntgc', h_re, c_re.astype(f32))
         - jnp.einsum('ntgp,gcp->ntgc', h_im, c_im.astype(f32)))
    y = y.reshape(n, t, SSM_WIDTH) + d_skip.astype(f32) * u32.reshape(n, t, SSM_WIDTH)
    return y.astype(u.dtype), h_re[:, -1], h_im[:, -1]


def _s5_layer(x, h0_re, h0_im, g_pre, g_post, w_in, log_dt, a_re, a_im, b_re, b_im,
              c_re, c_im, d_skip, w_glu, b_glu, w_out):
    h = _rms_norm(x, g_pre)
    uz = h @ w_in
    u, z = uz[..., :SSM_WIDTH], uz[..., SSM_WIDTH:]
    y, hr, hi = _s5_scan(u, h0_re, h0_im, log_dt, a_re, a_im, b_re, b_im, c_re, c_im, d_skip)
    y = jax.nn.gelu(y)
    y = y * jax.nn.sigmoid(y @ w_glu + b_glu)
    y = y * jax.nn.silu(z)
    return x + _rms_norm(y @ w_out, g_post), hr, hi


def _stick_break_weights(z, mask):
    log_fail = jnp.where(mask, jax.nn.log_sigmoid(-z), 0.0)
    later = lax.cumsum(log_fail, axis=z.ndim - 1, reverse=True) - log_fail
    return jnp.where(mask, jnp.exp(jax.nn.log_sigmoid(z) + later), 0.0)


def _stick_break_prompt(q, k, v, bias):
    b, s, h, dh = q.shape
    nb = s // Q_BLOCK
    scale = HEAD_DIM ** -0.5
    bias32 = bias.astype(jnp.float32)[None, :, None, None]
    k_pos = jnp.arange(s)
    q_blocks = q.reshape(b, nb, Q_BLOCK, h, dh).transpose(1, 0, 2, 3, 4)
    starts = jnp.arange(nb) * Q_BLOCK

    def block(args):
        qb, start = args
        z = jnp.einsum('bqhd,bkhd->bhqk', qb, k, preferred_element_type=jnp.float32) * scale + bias32
        q_pos = start + jnp.arange(Q_BLOCK)
        mask = k_pos[None, :] < q_pos[:, None]
        a = _stick_break_weights(z, mask)
        return jnp.einsum('bhqk,bkhd->bqhd', a.astype(v.dtype), v)

    o = lax.map(block, (q_blocks, starts))
    return o.transpose(1, 0, 2, 3, 4).reshape(b, s, h * dh)


def _stick_break_sample(q, k_past, v_past, k_new, v_new, bias):
    n, t, h, dh = q.shape
    past = k_past.shape[1]
    scale = HEAD_DIM ** -0.5
    bias32 = bias.astype(jnp.float32)[None, :, None, None]
    z = jnp.concatenate([
        jnp.einsum('bqhd,bkhd->bhqk', q, k_past, preferred_element_type=jnp.float32),
        jnp.einsum('bqhd,bkhd->bhqk', q, k_new, preferred_element_type=jnp.float32)], axis=-1) * scale + bias32
    q_pos = past + jnp.arange(t)
    k_pos = jnp.arange(past + t)
    mask = k_pos[None, :] < q_pos[:, None]
    a = _stick_break_weights(z, mask)
    o = (jnp.einsum('bhqk,bkhd->bqhd', a[..., :past].astype(v_past.dtype), v_past)
         + jnp.einsum('bhqk,bkhd->bqhd', a[..., past:].astype(v_new.dtype), v_new))
    return o.reshape(n, t, h * dh)


def _shared_kv(x, g, w_kv):
    h = _rms_norm(x, g)
    kv = h @ w_kv
    n, t, _ = x.shape
    k = kv[..., :ATT_WIDTH].reshape(n, t, N_HEADS, HEAD_DIM)
    v = kv[..., ATT_WIDTH:].reshape(n, t, N_HEADS, HEAD_DIM)
    return k, v


def _sb_layer(x, attend, g_pre, g_post, w_in, w_out):
    h = _rms_norm(x, g_pre)
    qg = h @ w_in
    n, t, _ = x.shape
    q = qg[..., :ATT_WIDTH].reshape(n, t, N_HEADS, HEAD_DIM)
    g = qg[..., ATT_WIDTH:]
    o = attend(q)
    return x + _rms_norm((o * jax.nn.silu(g)) @ w_out, g_post)


def setup_inputs(seed: int = 0) -> dict:
    key = jax.random.key(seed)
    ks = jax.random.split(key, 28)
    f32 = jnp.float32
    n_pages = PAST_LEN // PAGE_SIZE
    n_used = DEC_BATCH * n_pages
    n_phys = n_used + max(1, n_used // 4)
    page_table = jax.random.permutation(ks[6], n_phys)[:n_used].reshape(DEC_BATCH, n_pages).astype(jnp.int32)
    na, nbl = N_A_LAYERS, N_B_LAYERS

    def nrm(k, shape, scale):
        return jax.random.normal(k, shape, f32) * scale

    a_im0 = jnp.pi * jnp.arange(SSM_STATE, dtype=f32)
    return {
        'x_prompt': nrm(ks[0], (BATCH, SEQ, D_MODEL), 1.0),
        'x_sample': nrm(ks[1], (DEC_BATCH, DEC_SEQ, D_MODEL), 1.0),
        'state_ssm_re': nrm(ks[2], (na, DEC_BATCH, N_GROUPS, SSM_STATE), 0.5),
        'state_ssm_im': nrm(ks[3], (na, DEC_BATCH, N_GROUPS, SSM_STATE), 0.5),
        'cache_k': nrm(ks[4], (n_phys, PAGE_SIZE, N_HEADS, HEAD_DIM), 1.0),
        'cache_v': nrm(ks[5], (n_phys, PAGE_SIZE, N_HEADS, HEAD_DIM), 1.0),
        'page_table': page_table,
        'a_norm_pre': 1.0 + nrm(ks[7], (na, D_MODEL), 0.01),
        'a_norm_post': 1.0 + nrm(ks[8], (na, D_MODEL), 0.01),
        'a_w_in': nrm(ks[9], (na, D_MODEL, 2 * SSM_WIDTH), D_MODEL ** -0.5),
        'a_log_dt': jax.random.uniform(ks[10], (na, N_GROUPS), f32, math.log(DT_MIN), math.log(DT_MAX)),
        'a_A_re': -0.5 + nrm(ks[11], (na, N_GROUPS, SSM_STATE), 0.01),
        'a_A_im': a_im0 + nrm(ks[12], (na, N_GROUPS, SSM_STATE), 0.01),
        'a_B_re': nrm(ks[13], (na, N_GROUPS, SSM_STATE, SSM_GROUP), (2 * SSM_GROUP) ** -0.5),
        'a_B_im': nrm(ks[14], (na, N_GROUPS, SSM_STATE, SSM_GROUP), (2 * SSM_GROUP) ** -0.5),
        'a_C_re': nrm(ks[15], (na, N_GROUPS, SSM_GROUP, SSM_STATE), (2 * SSM_STATE) ** -0.5),
        'a_C_im': nrm(ks[16], (na, N_GROUPS, SSM_GROUP, SSM_STATE), (2 * SSM_STATE) ** -0.5),
        'a_D': nrm(ks[17], (na, SSM_WIDTH), 1.0),
        'a_w_glu': nrm(ks[18], (na, SSM_WIDTH, SSM_WIDTH), SSM_WIDTH ** -0.5),
        'a_b_glu': nrm(ks[19], (na, SSM_WIDTH), 0.01),
        'a_w_out': nrm(ks[20], (na, SSM_WIDTH, D_MODEL), SSM_WIDTH ** -0.5),
        'kv_norm': 1.0 + nrm(ks[21], (D_MODEL,), 0.01),
        'w_kv': nrm(ks[22], (D_MODEL, 2 * ATT_WIDTH), D_MODEL ** -0.5),
        'b_norm_pre': 1.0 + nrm(ks[23], (nbl, D_MODEL), 0.01),
        'b_norm_post': 1.0 + nrm(ks[24], (nbl, D_MODEL), 0.01),
        'b_w_in': nrm(ks[25], (nbl, D_MODEL, 2 * ATT_WIDTH), D_MODEL ** -0.5),
        'b_logit_bias': SB_BIAS_INIT + nrm(ks[27], (nbl, N_HEADS), 0.5),
        'b_w_out': nrm(ks[26], (nbl, ATT_WIDTH, D_MODEL), ATT_WIDTH ** -0.5),
    }


def reference(x_prompt, x_sample, state_ssm_re, state_ssm_im, cache_k, cache_v, page_table,
              a_norm_pre, a_norm_post, a_w_in, a_log_dt, a_A_re, a_A_im, a_B_re, a_B_im,
              a_C_re, a_C_im, a_D, a_w_glu, a_b_glu, a_w_out, kv_norm, w_kv,
              b_norm_pre, b_norm_post, b_w_in, b_logit_bias, b_w_out):
    xp, xs = x_prompt, x_sample
    p_re, p_im, s_re, s_im = [], [], [], []
    for layer in range(DEPTH):
        if layer < N_A_LAYERS:
            i = layer
            prm = (a_norm_pre[i], a_norm_post[i], a_w_in[i], a_log_dt[i], a_A_re[i], a_A_im[i],
                   a_B_re[i], a_B_im[i], a_C_re[i], a_C_im[i], a_D[i], a_w_glu[i], a_b_glu[i], a_w_out[i])
            xp, hr, hi = _s5_layer(xp, None, None, *prm)
            xs, sr, si = _s5_layer(xs, state_ssm_re[i], state_ssm_im[i], *prm)
            p_re.append(hr)
            p_im.append(hi)
            s_re.append(sr)
            s_im.append(si)
        else:
            if layer == N_A_LAYERS:
                k_p, v_p = _shared_kv(xp, kv_norm, w_kv)
                k_s, v_s = _shared_kv(xs, kv_norm, w_kv)
                n_seq = page_table.shape[0]
                k_past = cache_k[page_table].reshape(n_seq, -1, N_HEADS, HEAD_DIM)
                v_past = cache_v[page_table].reshape(n_seq, -1, N_HEADS, HEAD_DIM)
            j = layer - N_A_LAYERS
            bias_j = b_logit_bias[j]
            xp = _sb_layer(xp, lambda q: _stick_break_prompt(q, k_p, v_p, bias_j),
                           b_norm_pre[j], b_norm_post[j], b_w_in[j], b_w_out[j])
            xs = _sb_layer(xs, lambda q: _stick_break_sample(q, k_past, v_past, k_s, v_s, bias_j),
                           b_norm_pre[j], b_norm_post[j], b_w_in[j], b_w_out[j])
    return (xp, xs, jnp.stack(p_re), jnp.stack(p_im), k_p, v_p,
            jnp.stack(s_re), jnp.stack(s_im), k_s, v_s)
```

```python
import functools

import jax
import jax.numpy as jnp
from jax import lax
from jax.experimental import pallas as pl
from jax.experimental.pallas import tpu as pltpu

F32 = jnp.float32
BF16 = jnp.bfloat16

RMS_EPS = 1e-6
SSM_GROUP = 16
SSM_STATE = 64
SSM_CHUNK = 16
SSM_GROUP_BLOCK = 8
HEAD_DIM = 64
LANES = 128
HEADS_PER_LANE_TILE = LANES // HEAD_DIM
ATT_BLOCK = 256
PAGE_SIZE = 128
VMEM_LIMIT = 48 * 1024 * 1024


def _params(semantics):
    return pltpu.CompilerParams(dimension_semantics=semantics, vmem_limit_bytes=VMEM_LIMIT)


def _rms_scale(x):
    return lax.rsqrt(jnp.mean(x * x, axis=-1, keepdims=True) + RMS_EPS)


def _norm_matmul_kernel(x_ref, g_ref, w_ref, *o_refs, splits):
    x = x_ref[...]
    h = (x * _rms_scale(x)) * g_ref[...]
    acc = jnp.dot(h.astype(BF16), w_ref[...], preferred_element_type=F32)
    for o_ref, (lo, hi, scale) in zip(o_refs, splits):
        v = acc[:, lo:hi]
        if scale != 1.0:
            v = v * scale
        o_ref[...] = v.astype(o_ref.dtype)


def _norm_matmul(x, g, w, outs, block_rows):
    m, d = x.shape
    n = w.shape[1]
    tm = min(block_rows, m)
    splits = tuple((lo, hi, scale) for lo, hi, scale, _ in outs)
    return pl.pallas_call(
        functools.partial(_norm_matmul_kernel, splits=splits),
        grid=(m // tm,),
        in_specs=[pl.BlockSpec((tm, d), lambda i: (i, 0)),
                  pl.BlockSpec((1, d), lambda i: (0, 0)),
                  pl.BlockSpec((d, n), lambda i: (0, 0))],
        out_specs=[pl.BlockSpec((tm, hi - lo), lambda i: (i, 0)) for lo, hi, _, _ in outs],
        out_shape=[jax.ShapeDtypeStruct((m, hi - lo), dt) for lo, hi, _, dt in outs],
        compiler_params=_params(("parallel",)),
        name="norm_matmul",
    )(x, g.reshape(1, d), w)


def _ssm_prep_kernel(colp_ref, rowp_ref, ct_re_ref, ct_im_ref, bt_re_ref, bt_im_ref,
                     w1_ref, w2_ref, coef_ref):
    n_state2 = 2 * SSM_STATE
    n_in = SSM_CHUNK * SSM_GROUP
    are_c = colp_ref[0, :, 0:1]
    aim_c = colp_ref[0, :, 1:2]
    dt_c = jnp.exp(colp_ref[0, :, 2:3])
    tau_l = (lax.broadcasted_iota(jnp.int32, (1, n_in), 1) // SSM_GROUP).astype(F32)
    row_is_re = lax.broadcasted_iota(jnp.int32, (n_state2, 1), 0) < SSM_STATE
    ct_re = ct_re_ref[0]
    ct_im = ct_im_ref[0]

    def c_times_lam_pow(tau):
        mag = jnp.exp(tau * (dt_c * are_c))
        ang = tau * (dt_c * aim_c)
        pr = mag * jnp.cos(ang)
        pi = mag * jnp.sin(ang)
        return ct_re * pr - ct_im * pi, ct_re * pi + ct_im * pr

    g_re, g_im = c_times_lam_pow(tau_l)
    rhs_k = jnp.where(row_is_re, g_re, g_im)
    q_re, q_im = c_times_lam_pow(tau_l + 1.0)
    w2_ref[0] = jnp.where(row_is_re, q_re, -q_im).astype(w2_ref.dtype)

    are_r = rowp_ref[0, 0:1, :]
    aim_r = rowp_ref[0, 1:2, :]
    dt_r = jnp.exp(rowp_ref[0, 2:3, :])
    lane_is_re = lax.broadcasted_iota(jnp.int32, (1, n_state2), 1) < SSM_STATE

    def lam_pow_row(tau):
        mag = jnp.exp(tau * (dt_r * are_r))
        ang = tau * (dt_r * aim_r)
        return mag * jnp.cos(ang), mag * jnp.sin(ang)

    lr, li = lam_pow_row(1.0)
    den = are_r * are_r + aim_r * aim_r
    w_re = ((lr - 1.0) * are_r + li * aim_r) / den
    w_im = (li * are_r - (lr - 1.0) * aim_r) / den
    bt_re = bt_re_ref[0]
    bt_im = bt_im_ref[0]
    bb_re = w_re * bt_re - w_im * bt_im
    bb_im = w_re * bt_im + w_im * bt_re

    lhs_k = jnp.where(lane_is_re, bb_re[0:SSM_GROUP], -bb_im[0:SSM_GROUP])
    kk = jnp.dot(lhs_k, rhs_k, preferred_element_type=F32, precision=lax.Precision.HIGHEST)
    lane_in = lax.broadcasted_iota(jnp.int32, (1, n_in), 1)
    for s in range(SSM_CHUNK):
        shifted = kk if s == 0 else pltpu.roll(kk, s * SSM_GROUP, 1)
        piece = jnp.where(lane_in >= s * SSM_GROUP, shifted, 0.0)
        w1_ref[0, s * SSM_GROUP:(s + 1) * SSM_GROUP, 0:n_in] = piece.astype(w1_ref.dtype)

    tau_row = (SSM_CHUNK - 1 - lax.broadcasted_iota(jnp.int32, (n_in, 1), 0) // SSM_GROUP).astype(F32)
    pr_r, pi_r = lam_pow_row(tau_row)
    p_re = bb_re * pr_r - bb_im * pi_r
    p_im = bb_re * pi_r + bb_im * pr_r
    w1_ref[0, :, n_in:n_in + n_state2] = jnp.where(lane_is_re, p_re, p_im).astype(w1_ref.dtype)
    w1_ref[0, :, n_in + n_state2:n_in + 2 * n_state2] = jnp.where(lane_is_re, p_im, p_re).astype(w1_ref.dtype)

    lr_c, li_c = lam_pow_row(float(SSM_CHUNK))
    lr_h, li_h = lam_pow_row(float(SSM_CHUNK // 2))
    zero = jnp.zeros_like(lr_c)
    coef_ref[0] = jnp.concatenate(
        [lr_c, jnp.where(lane_is_re, -li_c, li_c), lr_h, jnp.where(lane_is_re, -li_h, li_h),
         zero, zero, zero, zero], axis=0)


def _ssm_prep(log_dt, a_re, a_im, b_re, b_im, c_re, c_im):
    g, p = a_re.shape
    c = b_re.shape[-1]
    n_in = SSM_CHUNK * c
    dup = lambda a: jnp.concatenate([a, a], axis=-1)
    dt_b = jnp.broadcast_to(log_dt[:, None], (g, 2 * p))
    cols = jnp.stack([dup(a_re), dup(a_im), dt_b] + [jnp.zeros((g, 2 * p), F32)] * 5, axis=-1)
    rows = jnp.stack([dup(a_re), dup(a_im), dt_b] + [jnp.zeros((g, 2 * p), F32)] * 5, axis=1)
    ct = lambda a: jnp.tile(jnp.swapaxes(a, 1, 2), (1, 2, SSM_CHUNK))
    bt = lambda a: jnp.tile(jnp.swapaxes(a, 1, 2), (1, SSM_CHUNK, 2))
    spec3 = lambda s1, s2: pl.BlockSpec((1, s1, s2), lambda i: (i, 0, 0))
    w1, w2, coef = pl.pallas_call(
        _ssm_prep_kernel,
        grid=(g,),
        in_specs=[spec3(2 * p, 8), spec3(8, 2 * p), spec3(2 * p, n_in), spec3(2 * p, n_in),
                  spec3(n_in, 2 * p), spec3(n_in, 2 * p)],
        out_specs=[spec3(n_in, n_in + 4 * p), spec3(2 * p, n_in), spec3(8, 2 * p)],
        out_shape=[jax.ShapeDtypeStruct((g, n_in, n_in + 4 * p), BF16),
                   jax.ShapeDtypeStruct((g, 2 * p, n_in), BF16),
                   jax.ShapeDtypeStruct((g, 8, 2 * p), F32)],
        compiler_params=_params(("parallel",)),
        name="ssm_prep",
    )(cols, rows, ct(c_re), ct(c_im), bt(b_re), bt(b_im))
    return w1, w2, jnp.swapaxes(coef, 0, 1)[:4]


def _ssm_seq_kernel(u_ref, w1_ref, w2_ref, coef_ref, y_ref, hfin_ref, s_ref, ss_ref):
    gb = SSM_GROUP_BLOCK
    nk = u_ref.shape[2]
    n_in = u_ref.shape[3]
    n_state2 = 2 * SSM_STATE
    for g in range(gb):
        r = jnp.dot(u_ref[0, g], w1_ref[g], preferred_element_type=F32)
        y_ref[0, g] = r[:, 0:n_in]
        s_ref[pl.ds(g, nk, stride=gb), :] = r[:, n_in:n_in + n_state2]
        ss_ref[pl.ds(g, nk, stride=gb), :] = r[:, n_in + n_state2:n_in + 2 * n_state2]
    a = coef_ref[0]
    b = coef_ref[1]

    def step(k, carry):
        x, xs = carry
        r0 = pl.multiple_of(k * gb, gb)
        add = s_ref[pl.ds(r0, gb), :]
        adds = ss_ref[pl.ds(r0, gb), :]
        s_ref[pl.ds(r0, gb), :] = x
        return a * x + b * xs + add, a * xs - b * x + adds

    zero = jnp.zeros((gb, n_state2), F32)
    x_fin, _ = lax.fori_loop(0, nk, step, (zero, zero))
    hfin_ref[0] = x_fin
    for g in range(gb):
        h_prev = s_ref[pl.ds(g, nk, stride=gb), :]
        y_ref[0, g] += jnp.dot(h_prev.astype(BF16), w2_ref[g], preferred_element_type=F32)


def _ssm_seq(u2, w1, w2, coef):
    b, g, nk, n_in = u2.shape
    gb = SSM_GROUP_BLOCK
    n_state2 = 2 * SSM_STATE
    return pl.pallas_call(
        _ssm_seq_kernel,
        grid=(b, g // gb),
        in_specs=[pl.BlockSpec((1, gb, nk, n_in), lambda i, j: (i, j, 0, 0)),
                  pl.BlockSpec((gb, n_in, n_in + 2 * n_state2), lambda i, j: (j, 0, 0)),
                  pl.BlockSpec((gb, n_state2, n_in), lambda i, j: (j, 0, 0)),
                  pl.BlockSpec((4, gb, n_state2), lambda i, j: (0, j, 0))],
        out_specs=[pl.BlockSpec((1, gb, nk, n_in), lambda i, j: (i, j, 0, 0)),
                   pl.BlockSpec((1, gb, n_state2), lambda i, j: (i, j, 0))],
        out_shape=[jax.ShapeDtypeStruct((b, g, nk, n_in), F32),
                   jax.ShapeDtypeStruct((b, g, n_state2), F32)],
        scratch_shapes=[pltpu.VMEM((nk * gb, n_state2), F32), pltpu.VMEM((nk * gb, n_state2), F32)],
        compiler_params=_params(("parallel", "parallel")),
        name="ssm_seq",
    )(u2, w1, w2, coef)


def _ssm_step_kernel(u_ref, h0_ref, h0s_ref, w1_ref, w2_ref, coef_ref, y_ref, hfin_ref):
    gb = u_ref.shape[0]
    n_half = u_ref.shape[2]
    n_in = 2 * n_half
    n_state2 = 2 * SSM_STATE
    for g in range(gb):
        u = u_ref[g]
        h0 = h0_ref[g]
        t_half = w1_ref[g, 0:n_half, 0:n_half]
        p_half = w1_ref[g, n_half:n_in, n_in:n_in + n_state2]
        q_half = w2_ref[g, :, 0:n_half]
        y_ref[g] = (jnp.dot(u, t_half, preferred_element_type=F32)
                    + jnp.dot(h0.astype(BF16), q_half, preferred_element_type=F32))
        hfin_ref[g] = (coef_ref[2, g:g + 1, :] * h0 + coef_ref[3, g:g + 1, :] * h0s_ref[g]
                       + jnp.dot(u, p_half, preferred_element_type=F32))


def _ssm_step(u8, h0, h0s, w1, w2, coef):
    g, n, n_half = u8.shape
    gb = SSM_GROUP_BLOCK
    n_state2 = 2 * SSM_STATE
    n_in = 2 * n_half
    blk = lambda s1, s2: pl.BlockSpec((gb, s1, s2), lambda j: (j, 0, 0))
    return pl.pallas_call(
        _ssm_step_kernel,
        grid=(g // gb,),
        in_specs=[blk(n, n_half), blk(n, n_state2), blk(n, n_state2),
                  blk(n_in, n_in + 2 * n_state2), blk(n_state2, n_in),
                  pl.BlockSpec((4, gb, n_state2), lambda j: (0, j, 0))],
        out_specs=[blk(n, n_half), blk(n, n_state2)],
        out_shape=[jax.ShapeDtypeStruct((g, n, n_half), F32), jax.ShapeDtypeStruct((g, n, n_state2), F32)],
        compiler_params=_params(("parallel",)),
        name="ssm_step",
    )(u8, h0, h0s, w1, w2, coef)


def _s5_post_kernel(y_ref, u_ref, z_ref, x_ref, d_ref, wg_ref, bg_ref, wo_ref, gp_ref, o_ref):
    y = jax.nn.gelu(y_ref[...] + d_ref[...] * u_ref[...])
    gate = jnp.dot(y.astype(BF16), wg_ref[...], preferred_element_type=F32) + bg_ref[...]
    y = y * jax.nn.sigmoid(gate)
    z = z_ref[...]
    y = y * (z * jax.nn.sigmoid(z))
    o = jnp.dot(y.astype(BF16), wo_ref[...], preferred_element_type=F32)
    o_ref[...] = x_ref[...] + (o * _rms_scale(o)) * gp_ref[...]


def _s5_post(y, u, z, x, d_skip, w_glu, b_glu, w_out, g_post, block_rows):
    m, d = x.shape
    tm = min(block_rows, m)
    row = pl.BlockSpec((tm, d), lambda i: (i, 0))
    vec = pl.BlockSpec((1, d), lambda i: (0, 0))
    mat = pl.BlockSpec((d, d), lambda i: (0, 0))
    return pl.pallas_call(
        _s5_post_kernel,
        grid=(m // tm,),
        in_specs=[row, row, row, row, vec, mat, vec, mat, vec],
        out_specs=row,
        out_shape=jax.ShapeDtypeStruct((m, d), F32),
        compiler_params=_params(("parallel",)),
        name="s5_post",
    )(y, u, z, x, d_skip.reshape(1, d), w_glu, b_glu.reshape(1, d), w_out, g_post.reshape(1, d))


def _sb_post_kernel(o_ref, g_ref, x_ref, wo_ref, gp_ref, out_ref):
    g = g_ref[...]
    y = o_ref[...] * (g * jax.nn.sigmoid(g))
    r = jnp.dot(y.astype(BF16), wo_ref[...], preferred_element_type=F32)
    out_ref[...] = x_ref[...] + (r * _rms_scale(r)) * gp_ref[...]


def _sb_post(o, g, x, w_out, g_post, block_rows):
    m, d = x.shape
    tm = min(block_rows, m)
    row = pl.BlockSpec((tm, d), lambda i: (i, 0))
    return pl.pallas_call(
        _sb_post_kernel,
        grid=(m // tm,),
        in_specs=[row, row, row, pl.BlockSpec((d, d), lambda i: (0, 0)), pl.BlockSpec((1, d), lambda i: (0, 0))],
        out_specs=row,
        out_shape=jax.ShapeDtypeStruct((m, d), F32),
        compiler_params=_params(("parallel",)),
        name="sb_post",
    )(o, g, x, w_out, g_post.reshape(1, d))


def _suffix_mask(n):
    return (lax.broadcasted_iota(jnp.int32, (n, n), 0) > lax.broadcasted_iota(jnp.int32, (n, n), 1)).astype(BF16)


def _stick_break_block(z, mask, later_blocks, tri):
    softplus = jnp.maximum(z, 0.0) + jnp.log1p(jnp.exp(-jnp.abs(z)))
    log_fail = -softplus if mask is None else jnp.where(mask, -softplus, 0.0)
    hi = log_fail.astype(BF16)
    lo = (log_fail - hi.astype(F32)).astype(BF16)
    later = (jnp.dot(hi, tri, preferred_element_type=F32) + jnp.dot(lo, tri, preferred_element_type=F32))
    w = jnp.exp(z + log_fail + later + later_blocks)
    if mask is not None:
        w = jnp.where(mask, w, 0.0)
    return w, jnp.sum(log_fail, axis=1, keepdims=True)


def _attn_prompt_kernel(bias_ref, q_ref, k_ref, v_ref, o_ref):
    tq = q_ref.shape[1]
    tk = tq
    hp = pl.program_id(1)
    qi = pl.program_id(2)
    q = q_ref[0]
    lane = lax.broadcasted_iota(jnp.int32, (1, LANES), 1)
    row = lax.broadcasted_iota(jnp.int32, (tq, 1), 0)
    col = lax.broadcasted_iota(jnp.int32, (1, tk), 1)
    tri = _suffix_mask(tk)
    outs = []
    for hh in range(HEADS_PER_LANE_TILE):
        in_head = (lane >= hh * HEAD_DIM) & (lane < (hh + 1) * HEAD_DIM)
        qm = jnp.where(in_head, q, jnp.zeros_like(q))
        bias = bias_ref[hp * HEADS_PER_LANE_TILE + hh]

        def body(j, carry):
            o_acc, later_blocks = carry
            ks = pl.multiple_of((qi - j) * tk, tk)
            kb = k_ref[0, pl.ds(ks, tk), :]
            vb = v_ref[0, pl.ds(ks, tk), :]
            z = lax.dot_general(qm, kb, (((1,), (1,)), ((), ())), preferred_element_type=F32) + bias
            mask = (col + ks) < (row + qi * tq)
            w, tot = _stick_break_block(z, mask, later_blocks, tri)
            o_acc = o_acc + jnp.dot(w.astype(BF16), vb, preferred_element_type=F32)
            return o_acc, later_blocks + tot

        o_acc, _ = lax.fori_loop(0, qi + 1, body, (jnp.zeros((tq, LANES), F32), jnp.zeros((tq, 1), F32)))
        outs.append(o_acc)
    o_ref[0] = jnp.where(lane < HEAD_DIM, outs[0], outs[1])


def _attn_prompt(q, k, v, bias):
    b, s, hd = q.shape
    tq = min(ATT_BLOCK, s)
    qspec = pl.BlockSpec((1, tq, LANES), lambda i, h, j: (i, j, h))
    kspec = pl.BlockSpec((1, s, LANES), lambda i, h, j: (i, 0, h))
    return pl.pallas_call(
        _attn_prompt_kernel,
        grid=(b, hd // LANES, s // tq),
        in_specs=[pl.BlockSpec(memory_space=pltpu.SMEM), qspec, kspec, kspec],
        out_specs=qspec,
        out_shape=jax.ShapeDtypeStruct((b, s, hd), F32),
        compiler_params=_params(("parallel", "parallel", "arbitrary")),
        name="attn_prompt",
    )(bias, q, k, v)


def _attn_paged_kernel(pt_ref, q_ref, knew_ref, vnew_ref, kpage_ref, vpage_ref, bias_ref, o_ref,
                       qbd_ref, acc_ref, later_ref):
    del pt_ref
    j = pl.program_id(1)
    nq = q_ref.shape[0]
    hd = q_ref.shape[1]
    n_heads = hd // HEAD_DIM
    rows = n_heads * nq
    tk = kpage_ref.shape[1]
    row = lax.broadcasted_iota(jnp.int32, (rows, 1), 0)
    tri = _suffix_mask(tk)

    def visit(kf, vf, mask):
        z = lax.dot_general(qbd_ref[...], kf.astype(BF16), (((1,), (1,)), ((), ())),
                            preferred_element_type=F32) + bias_ref[...]
        w, tot = _stick_break_block(z, mask, later_ref[...], tri)
        acc_ref[...] += jnp.dot(w.astype(BF16), vf.astype(BF16), preferred_element_type=F32)
        later_ref[...] += tot

    @pl.when(j == 0)
    def _():
        lane = lax.broadcasted_iota(jnp.int32, (1, hd), 1)
        q_rep = jnp.concatenate([q_ref[...]] * n_heads, axis=0)
        qbd_ref[...] = jnp.where(lane // HEAD_DIM == row // nq, q_rep, 0.0).astype(BF16)
        acc_ref[...] = jnp.zeros_like(acc_ref)
        later_ref[...] = jnp.zeros_like(later_ref)
        col = lax.broadcasted_iota(jnp.int32, (1, tk), 1)
        visit(knew_ref[0], vnew_ref[0], col < row % nq)

    @pl.when(j > 0)
    def _():
        visit(kpage_ref[0], vpage_ref[0], None)

    @pl.when(j == pl.num_programs(1) - 1)
    def _():
        lane = lax.broadcasted_iota(jnp.int32, (1, hd), 1)
        out = jnp.zeros((nq, hd), F32)
        for h in range(n_heads):
            out = out + jnp.where(lane // HEAD_DIM == h, acc_ref[h * nq:(h + 1) * nq, :], 0.0)
        o_ref[...] = out


def _attn_paged(q, k_new, v_new, cache_k, cache_v, page_table, bias, n_new):
    n_seq, n_pages = page_table.shape
    hd = q.shape[1]
    n_heads = hd // HEAD_DIM
    rows = n_heads * n_new
    page = cache_k.shape[1]
    seq_rows = pl.BlockSpec((n_new, hd), lambda n, j, pt: (n, 0))
    new_spec = pl.BlockSpec((1, page, hd), lambda n, j, pt: (n, 0, 0))
    page_spec = pl.BlockSpec((1, page, hd), lambda n, j, pt: (pt[n, n_pages - jnp.maximum(j, 1)], 0, 0))
    grid_spec = pltpu.PrefetchScalarGridSpec(
        num_scalar_prefetch=1,
        grid=(n_seq, n_pages + 1),
        in_specs=[seq_rows, new_spec, new_spec, page_spec, page_spec,
                  pl.BlockSpec((rows, 1), lambda n, j, pt: (0, 0))],
        out_specs=seq_rows,
        scratch_shapes=[pltpu.VMEM((rows, hd), BF16), pltpu.VMEM((rows, hd), F32), pltpu.VMEM((rows, 1), F32)],
    )
    return pl.pallas_call(
        _attn_paged_kernel,
        grid_spec=grid_spec,
        out_shape=jax.ShapeDtypeStruct((n_seq * n_new, hd), F32),
        compiler_params=_params(("parallel", "arbitrary")),
        name="attn_paged",
    )(page_table, q, k_new, v_new, cache_k, cache_v, jnp.repeat(bias, n_new).reshape(rows, 1))


def _s5_layer(xp, xs, n_prompt, h0_re, h0_im, prm):
    (g_pre, g_post, w_in, log_dt, a_re, a_im, b_re, b_im, c_re, c_im, d_skip, w_glu, b_glu, w_out) = prm
    d = xp.shape[1]
    n_groups, n_state = a_re.shape
    gc = b_re.shape[-1]
    width = n_groups * gc
    w1, w2, coef = _ssm_prep(log_dt, a_re, a_im, b_re, b_im, c_re, c_im)
    w_in_b, w_glu_b, w_out_b = w_in.astype(BF16), w_glu.astype(BF16), w_out.astype(BF16)
    uz_outs = [(0, width, 1.0, F32), (width, 2 * width, 1.0, F32)]

    u, z = _norm_matmul(xp, g_pre, w_in_b, uz_outs, 512)
    seq = xp.shape[0] // n_prompt
    nk = seq // SSM_CHUNK
    u2 = (u.reshape(n_prompt, nk, SSM_CHUNK, n_groups, gc).transpose(0, 3, 1, 2, 4)
          .reshape(n_prompt, n_groups, nk, SSM_CHUNK * gc).astype(BF16))
    y2, hfin = _ssm_seq(u2, w1, w2, coef)
    y = (y2.reshape(n_prompt, n_groups, nk, SSM_CHUNK, gc).transpose(0, 2, 3, 1, 4).reshape(xp.shape[0], width))
    xp = _s5_post(y, u, z, xp, d_skip, w_glu_b, b_glu, w_out_b, g_post, 512)

    n_seq = h0_re.shape[0]
    t_new = xs.shape[0] // n_seq
    us, zs = _norm_matmul(xs, g_pre, w_in_b, uz_outs, 512)
    u8 = (us.reshape(n_seq, t_new, n_groups, gc).transpose(2, 0, 1, 3)
          .reshape(n_groups, n_seq, t_new * gc).astype(BF16))
    h0 = jnp.concatenate([h0_re, h0_im], axis=-1).transpose(1, 0, 2)
    h0s = jnp.concatenate([h0_im, h0_re], axis=-1).transpose(1, 0, 2)
    y8, hf8 = _ssm_step(u8, h0, h0s, w1, w2, coef)
    ys = y8.reshape(n_groups, n_seq, t_new, gc).transpose(1, 2, 0, 3).reshape(xs.shape[0], width)
    xs = _s5_post(ys, us, zs, xs, d_skip, w_glu_b, b_glu, w_out_b, g_post, 512)
    hf8 = hf8.transpose(1, 0, 2)
    return (xp, xs, hfin[..., :n_state], hfin[..., n_state:], hf8[..., :n_state], hf8[..., n_state:])


def kernel(x_prompt, x_sample, state_ssm_re, state_ssm_im, cache_k, cache_v, page_table, a_norm_pre, a_norm_post, a_w_in, a_log_dt, a_A_re, a_A_im, a_B_re, a_B_im, a_C_re, a_C_im, a_D, a_w_glu, a_b_glu, a_w_out, kv_norm, w_kv, b_norm_pre, b_norm_post, b_w_in, b_logit_bias, b_w_out):
    n_prompt, seq, d = x_prompt.shape
    n_seq, t_new, _ = x_sample.shape
    n_a = a_w_in.shape[0]
    n_b = b_w_in.shape[0]
    att = w_kv.shape[1] // 2
    n_heads = att // HEAD_DIM
    assert t_new * 2 == SSM_CHUNK and cache_k.shape[1] == PAGE_SIZE and HEAD_DIM ** -0.5 == 0.125

    xp = x_prompt.reshape(n_prompt * seq, d)
    xs = x_sample.reshape(n_seq * t_new, d)
    p_re, p_im, s_re, s_im = [], [], [], []
    for i in range(n_a):
        prm = (a_norm_pre[i], a_norm_post[i], a_w_in[i], a_log_dt[i], a_A_re[i], a_A_im[i], a_B_re[i], a_B_im[i],
               a_C_re[i], a_C_im[i], a_D[i], a_w_glu[i], a_b_glu[i], a_w_out[i])
        xp, xs, hr, hi, sr, si = _s5_layer(xp, xs, n_prompt, state_ssm_re[i], state_ssm_im[i], prm)
        p_re.append(hr)
        p_im.append(hi)
        s_re.append(sr)
        s_im.append(si)

    w_kv_b = w_kv.astype(BF16)
    kv_outs = [(0, att, 1.0, F32), (att, 2 * att, 1.0, F32), (0, att, 1.0, BF16), (att, 2 * att, 1.0, BF16)]
    k_p, v_p, k_pb, v_pb = _norm_matmul(xp, kv_norm, w_kv_b, kv_outs, 512)
    k_s, v_s = _norm_matmul(xs, kv_norm, w_kv_b, kv_outs[:2], 512)
    pad_new = lambda a: jnp.pad(a.reshape(n_seq, t_new, att), ((0, 0), (0, PAGE_SIZE - t_new), (0, 0)))
    k_s_pad, v_s_pad = pad_new(k_s), pad_new(v_s)
    cache_k3 = cache_k.reshape(cache_k.shape[0], PAGE_SIZE, att)
    cache_v3 = cache_v.reshape(cache_v.shape[0], PAGE_SIZE, att)

    scale = HEAD_DIM ** -0.5
    for j in range(n_b):
        w_in_b, w_out_b = b_w_in[j].astype(BF16), b_w_out[j].astype(BF16)
        q, g = _norm_matmul(xp, b_norm_pre[j], w_in_b, [(0, att, scale, BF16), (att, 2 * att, 1.0, F32)], 512)
        o = _attn_prompt(q.reshape(n_prompt, seq, att), k_pb.reshape(n_prompt, seq, att),
                         v_pb.reshape(n_prompt, seq, att), b_logit_bias[j])
        xp = _sb_post(o.reshape(n_prompt * seq, att), g, xp, w_out_b, b_norm_post[j], 512)
        qs, gs = _norm_matmul(xs, b_norm_pre[j], w_in_b, [(0, att, scale, F32), (att, 2 * att, 1.0, F32)], 512)
        os_ = _attn_paged(qs, k_s_pad, v_s_pad, cache_k3, cache_v3, page_table, b_logit_bias[j], t_new)
        xs = _sb_post(os_, gs, xs, w_out_b, b_norm_post[j], 512)

    return (xp.reshape(n_prompt, seq, d), xs.reshape(n_seq, t_new, d),
            jnp.stack(p_re), jnp.stack(p_im),
            k_p.reshape(n_prompt, seq, n_heads, HEAD_DIM), v_p.reshape(n_prompt, seq, n_heads, HEAD_DIM),
            jnp.stack(s_re), jnp.stack(s_im),
            k_s.reshape(n_seq, t_new, n_heads, HEAD_DIM), v_s.reshape(n_seq, t_new, n_heads, HEAD_DIM))
```

```python
import functools

import jax
import jax.numpy as jnp
from jax import lax
from jax.experimental import pallas as pl
from jax.experimental.pallas import tpu as pltpu

F32 = jnp.float32
BF16 = jnp.bfloat16

RMS_EPS = 1e-6
SSM_GROUP = 16
SSM_STATE = 64
SSM_CHUNK = 16
SSM_GROUP_BLOCK = 8
HEAD_DIM = 64
LANES = 128
HEADS_PER_LANE_TILE = LANES // HEAD_DIM
ATT_BLOCK = 256
Q_SUBS = 4
PAGE_SIZE = 128
PAGES_PER_STEP = 4
VMEM_LIMIT = 48 * 1024 * 1024


def _params(semantics):
    return pltpu.CompilerParams(dimension_semantics=semantics, vmem_limit_bytes=VMEM_LIMIT)


def _rms_scale(x):
    return lax.rsqrt(jnp.mean(x * x, axis=-1, keepdims=True) + RMS_EPS)


def _norm_matmul_kernel(x_ref, g_ref, w_ref, *o_refs, splits):
    x = x_ref[...]
    h = (x * _rms_scale(x)) * g_ref[...]
    acc = jnp.dot(h.astype(BF16), w_ref[...], preferred_element_type=F32)
    for o_ref, (lo, hi, scale) in zip(o_refs, splits):
        v = acc[:, lo:hi]
        if scale != 1.0:
            v = v * scale
        o_ref[...] = v.astype(o_ref.dtype)


def _norm_matmul(x, g, w, outs, block_rows):
    m, d = x.shape
    n = w.shape[1]
    tm = min(block_rows, m)
    splits = tuple((lo, hi, scale) for lo, hi, scale, _ in outs)
    return pl.pallas_call(
        functools.partial(_norm_matmul_kernel, splits=splits),
        grid=(m // tm,),
        in_specs=[pl.BlockSpec((tm, d), lambda i: (i, 0)),
                  pl.BlockSpec((1, d), lambda i: (0, 0)),
                  pl.BlockSpec((d, n), lambda i: (0, 0))],
        out_specs=[pl.BlockSpec((tm, hi - lo), lambda i: (i, 0)) for lo, hi, _, _ in outs],
        out_shape=[jax.ShapeDtypeStruct((m, hi - lo), dt) for lo, hi, _, dt in outs],
        compiler_params=_params(("parallel",)),
        name="norm_matmul",
    )(x, g.reshape(1, d), w)


def _kv_proj_kernel(x_ref, g_ref, wt_ref, kt_ref, vt_ref, ktb_ref, vtb_ref):
    x = x_ref[...]
    h = (x * _rms_scale(x)) * g_ref[...]
    acc = lax.dot_general(wt_ref[...], h.astype(BF16), (((1,), (1,)), ((), ())), preferred_element_type=F32)
    att = kt_ref.shape[1]
    n_blocks = ktb_ref.shape[2]
    for t_ref, b_ref, lo in ((kt_ref, ktb_ref, 0), (vt_ref, vtb_ref, att)):
        part = acc[lo:lo + att]
        t_ref[0] = part
        for c in range(n_blocks):
            blk = part[:, c * ATT_BLOCK:(c + 1) * ATT_BLOCK]
            b_ref[0, :, c] = blk.reshape(att // LANES, LANES, ATT_BLOCK).astype(b_ref.dtype)


def _kv_proj(x, g, wt, n_prompt, block_rows):
    m, d = x.shape
    att = wt.shape[0] // 2
    seq = m // n_prompt
    tm = min(block_rows, seq)
    steps = seq // tm
    n_blocks = tm // ATT_BLOCK
    t_spec = pl.BlockSpec((1, att, tm), lambda b, i: (b, 0, i))
    b_spec = pl.BlockSpec((1, att // LANES, n_blocks, LANES, ATT_BLOCK), lambda b, i: (b, 0, i, 0, 0))
    blocked = jax.ShapeDtypeStruct((n_prompt, att // LANES, seq // ATT_BLOCK, LANES, ATT_BLOCK), BF16)
    return pl.pallas_call(
        _kv_proj_kernel,
        grid=(n_prompt, steps),
        in_specs=[pl.BlockSpec((tm, d), lambda b, i: (b * steps + i, 0)),
                  pl.BlockSpec((1, d), lambda b, i: (0, 0)),
                  pl.BlockSpec((2 * att, d), lambda b, i: (0, 0))],
        out_specs=[t_spec, t_spec, b_spec, b_spec],
        out_shape=[jax.ShapeDtypeStruct((n_prompt, att, seq), F32)] * 2 + [blocked] * 2,
        compiler_params=_params(("parallel", "parallel")),
        name="kv_proj",
    )(x, g.reshape(1, d), wt)


def _ssm_prep_kernel(colp_ref, rowp_ref, ct_re_ref, ct_im_ref, bt_re_ref, bt_im_ref,
                     w1_ref, w2_ref, coef_ref):
    n_state2 = 2 * SSM_STATE
    n_in = SSM_CHUNK * SSM_GROUP
    are_c = colp_ref[0, :, 0:1]
    aim_c = colp_ref[0, :, 1:2]
    dt_c = jnp.exp(colp_ref[0, :, 2:3])
    tau_l = (lax.broadcasted_iota(jnp.int32, (1, n_in), 1) // SSM_GROUP).astype(F32)
    row_is_re = lax.broadcasted_iota(jnp.int32, (n_state2, 1), 0) < SSM_STATE
    ct_re = ct_re_ref[0]
    ct_im = ct_im_ref[0]

    def c_times_lam_pow(tau):
        mag = jnp.exp(tau * (dt_c * are_c))
        ang = tau * (dt_c * aim_c)
        pr = mag * jnp.cos(ang)
        pi = mag * jnp.sin(ang)
        return ct_re * pr - ct_im * pi, ct_re * pi + ct_im * pr

    g_re, g_im = c_times_lam_pow(tau_l)
    rhs_k = jnp.where(row_is_re, g_re, g_im)
    q_re, q_im = c_times_lam_pow(tau_l + 1.0)
    w2_ref[0] = jnp.where(row_is_re, q_re, -q_im).astype(w2_ref.dtype)

    are_r = rowp_ref[0, 0:1, :]
    aim_r = rowp_ref[0, 1:2, :]
    dt_r = jnp.exp(rowp_ref[0, 2:3, :])
    lane_is_re = lax.broadcasted_iota(jnp.int32, (1, n_state2), 1) < SSM_STATE

    def lam_pow_row(tau):
        mag = jnp.exp(tau * (dt_r * are_r))
        ang = tau * (dt_r * aim_r)
        return mag * jnp.cos(ang), mag * jnp.sin(ang)

    lr, li = lam_pow_row(1.0)
    den = are_r * are_r + aim_r * aim_r
    w_re = ((lr - 1.0) * are_r + li * aim_r) / den
    w_im = (li * are_r - (lr - 1.0) * aim_r) / den
    bt_re = bt_re_ref[0]
    bt_im = bt_im_ref[0]
    bb_re = w_re * bt_re - w_im * bt_im
    bb_im = w_re * bt_im + w_im * bt_re

    lhs_k = jnp.where(lane_is_re, bb_re[0:SSM_GROUP], -bb_im[0:SSM_GROUP])
    kk = jnp.dot(lhs_k, rhs_k, preferred_element_type=F32, precision=lax.Precision.HIGHEST)
    lane_in = lax.broadcasted_iota(jnp.int32, (1, n_in), 1)
    for s in range(SSM_CHUNK):
        shifted = kk if s == 0 else pltpu.roll(kk, s * SSM_GROUP, 1)
        piece = jnp.where(lane_in >= s * SSM_GROUP, shifted, 0.0)
        w1_ref[0, s * SSM_GROUP:(s + 1) * SSM_GROUP, 0:n_in] = piece.astype(w1_ref.dtype)

    tau_row = (SSM_CHUNK - 1 - lax.broadcasted_iota(jnp.int32, (n_in, 1), 0) // SSM_GROUP).astype(F32)
    pr_r, pi_r = lam_pow_row(tau_row)
    p_re = bb_re * pr_r - bb_im * pi_r
    p_im = bb_re * pi_r + bb_im * pr_r
    w1_ref[0, :, n_in:n_in + n_state2] = jnp.where(lane_is_re, p_re, p_im).astype(w1_ref.dtype)
    w1_ref[0, :, n_in + n_state2:n_in + 2 * n_state2] = jnp.where(lane_is_re, p_im, p_re).astype(w1_ref.dtype)

    lr_c, li_c = lam_pow_row(float(SSM_CHUNK))
    lr_h, li_h = lam_pow_row(float(SSM_CHUNK // 2))
    zero = jnp.zeros_like(lr_c)
    coef_ref[0] = jnp.concatenate(
        [lr_c, jnp.where(lane_is_re, -li_c, li_c), lr_h, jnp.where(lane_is_re, -li_h, li_h),
         zero, zero, zero, zero], axis=0)


def _ssm_prep(log_dt, a_re, a_im, b_re, b_im, c_re, c_im):
    g, p = a_re.shape
    c = b_re.shape[-1]
    n_in = SSM_CHUNK * c
    dup = lambda a: jnp.concatenate([a, a], axis=-1)
    dt_b = jnp.broadcast_to(log_dt[:, None], (g, 2 * p))
    cols = jnp.stack([dup(a_re), dup(a_im), dt_b] + [jnp.zeros((g, 2 * p), F32)] * 5, axis=-1)
    rows = jnp.stack([dup(a_re), dup(a_im), dt_b] + [jnp.zeros((g, 2 * p), F32)] * 5, axis=1)
    ct = lambda a: jnp.tile(jnp.swapaxes(a, 1, 2), (1, 2, SSM_CHUNK))
    bt = lambda a: jnp.tile(jnp.swapaxes(a, 1, 2), (1, SSM_CHUNK, 2))
    spec3 = lambda s1, s2: pl.BlockSpec((1, s1, s2), lambda i: (i, 0, 0))
    w1, w2, coef = pl.pallas_call(
        _ssm_prep_kernel,
        grid=(g,),
        in_specs=[spec3(2 * p, 8), spec3(8, 2 * p), spec3(2 * p, n_in), spec3(2 * p, n_in),
                  spec3(n_in, 2 * p), spec3(n_in, 2 * p)],
        out_specs=[spec3(n_in, n_in + 4 * p), spec3(2 * p, n_in), spec3(8, 2 * p)],
        out_shape=[jax.ShapeDtypeStruct((g, n_in, n_in + 4 * p), BF16),
                   jax.ShapeDtypeStruct((g, 2 * p, n_in), BF16),
                   jax.ShapeDtypeStruct((g, 8, 2 * p), F32)],
        compiler_params=_params(("parallel",)),
        name="ssm_prep",
    )(cols, rows, ct(c_re), ct(c_im), bt(b_re), bt(b_im))
    return w1, w2, jnp.swapaxes(coef, 0, 1)[:4]


def _ssm_seq_kernel(u_ref, w1_ref, w2_ref, coef_ref, y_ref, hfin_ref, s_ref, ss_ref):
    gb = SSM_GROUP_BLOCK
    nk = u_ref.shape[2]
    n_in = u_ref.shape[3]
    n_state2 = 2 * SSM_STATE
    for g in range(gb):
        r = jnp.dot(u_ref[0, g], w1_ref[g], preferred_element_type=F32)
        y_ref[0, g] = r[:, 0:n_in]
        s_ref[pl.ds(g, nk, stride=gb), :] = r[:, n_in:n_in + n_state2]
        ss_ref[pl.ds(g, nk, stride=gb), :] = r[:, n_in + n_state2:n_in + 2 * n_state2]
    a = coef_ref[0]
    b = coef_ref[1]

    def step(k, carry):
        x, xs = carry
        r0 = pl.multiple_of(k * gb, gb)
        add = s_ref[pl.ds(r0, gb), :]
        adds = ss_ref[pl.ds(r0, gb), :]
        s_ref[pl.ds(r0, gb), :] = x
        return a * x + b * xs + add, a * xs - b * x + adds

    zero = jnp.zeros((gb, n_state2), F32)
    x_fin, _ = lax.fori_loop(0, nk, step, (zero, zero))
    hfin_ref[0] = x_fin
    for g in range(gb):
        h_prev = s_ref[pl.ds(g, nk, stride=gb), :]
        y_ref[0, g] += jnp.dot(h_prev.astype(BF16), w2_ref[g], preferred_element_type=F32)


def _ssm_seq(u2, w1, w2, coef):
    b, g, nk, n_in = u2.shape
    gb = SSM_GROUP_BLOCK
    n_state2 = 2 * SSM_STATE
    return pl.pallas_call(
        _ssm_seq_kernel,
        grid=(b, g // gb),
        in_specs=[pl.BlockSpec((1, gb, nk, n_in), lambda i, j: (i, j, 0, 0)),
                  pl.BlockSpec((gb, n_in, n_in + 2 * n_state2), lambda i, j: (j, 0, 0)),
                  pl.BlockSpec((gb, n_state2, n_in), lambda i, j: (j, 0, 0)),
                  pl.BlockSpec((4, gb, n_state2), lambda i, j: (0, j, 0))],
        out_specs=[pl.BlockSpec((1, gb, nk, n_in), lambda i, j: (i, j, 0, 0)),
                   pl.BlockSpec((1, gb, n_state2), lambda i, j: (i, j, 0))],
        out_shape=[jax.ShapeDtypeStruct((b, g, nk, n_in), F32),
                   jax.ShapeDtypeStruct((b, g, n_state2), F32)],
        scratch_shapes=[pltpu.VMEM((nk * gb, n_state2), F32), pltpu.VMEM((nk * gb, n_state2), F32)],
        compiler_params=_params(("parallel", "parallel")),
        name="ssm_seq",
    )(u2, w1, w2, coef)


def _ssm_step_kernel(u_ref, h0_ref, h0s_ref, w1_ref, w2_ref, coef_ref, y_ref, hfin_ref):
    gb = u_ref.shape[0]
    n_half = u_ref.shape[2]
    n_in = 2 * n_half
    n_state2 = 2 * SSM_STATE
    for g in range(gb):
        u = u_ref[g]
        h0 = h0_ref[g]
        t_half = w1_ref[g, 0:n_half, 0:n_half]
        p_half = w1_ref[g, n_half:n_in, n_in:n_in + n_state2]
        q_half = w2_ref[g, :, 0:n_half]
        y_ref[g] = (jnp.dot(u, t_half, preferred_element_type=F32)
                    + jnp.dot(h0.astype(BF16), q_half, preferred_element_type=F32))
        hfin_ref[g] = (coef_ref[2, g:g + 1, :] * h0 + coef_ref[3, g:g + 1, :] * h0s_ref[g]
                       + jnp.dot(u, p_half, preferred_element_type=F32))


def _ssm_step(u8, h0, h0s, w1, w2, coef):
    g, n, n_half = u8.shape
    gb = SSM_GROUP_BLOCK
    n_state2 = 2 * SSM_STATE
    n_in = 2 * n_half
    blk = lambda s1, s2: pl.BlockSpec((gb, s1, s2), lambda j: (j, 0, 0))
    return pl.pallas_call(
        _ssm_step_kernel,
        grid=(g // gb,),
        in_specs=[blk(n, n_half), blk(n, n_state2), blk(n, n_state2),
                  blk(n_in, n_in + 2 * n_state2), blk(n_state2, n_in),
                  pl.BlockSpec((4, gb, n_state2), lambda j: (0, j, 0))],
        out_specs=[blk(n, n_half), blk(n, n_state2)],
        out_shape=[jax.ShapeDtypeStruct((g, n, n_half), F32), jax.ShapeDtypeStruct((g, n, n_state2), F32)],
        compiler_params=_params(("parallel",)),
        name="ssm_step",
    )(u8, h0, h0s, w1, w2, coef)


def _s5_post_kernel(y_ref, u_ref, z_ref, x_ref, d_ref, wg_ref, bg_ref, wo_ref, gp_ref, o_ref):
    y = jax.nn.gelu(y_ref[...] + d_ref[...] * u_ref[...])
    gate = jnp.dot(y.astype(BF16), wg_ref[...], preferred_element_type=F32) + bg_ref[...]
    y = y * jax.nn.sigmoid(gate)
    z = z_ref[...]
    y = y * (z * jax.nn.sigmoid(z))
    o = jnp.dot(y.astype(BF16), wo_ref[...], preferred_element_type=F32)
    o_ref[...] = x_ref[...] + (o * _rms_scale(o)) * gp_ref[...]


def _s5_post(y, u, z, x, d_skip, w_glu, b_glu, w_out, g_post, block_rows):
    m, d = x.shape
    tm = min(block_rows, m)
    row = pl.BlockSpec((tm, d), lambda i: (i, 0))
    vec = pl.BlockSpec((1, d), lambda i: (0, 0))
    mat = pl.BlockSpec((d, d), lambda i: (0, 0))
    return pl.pallas_call(
        _s5_post_kernel,
        grid=(m // tm,),
        in_specs=[row, row, row, row, vec, mat, vec, mat, vec],
        out_specs=row,
        out_shape=jax.ShapeDtypeStruct((m, d), F32),
        compiler_params=_params(("parallel",)),
        name="s5_post",
    )(y, u, z, x, d_skip.reshape(1, d), w_glu, b_glu.reshape(1, d), w_out, g_post.reshape(1, d))


def _sb_post_kernel(o_ref, g_ref, x_ref, wo_ref, gp_ref, out_ref):
    g = g_ref[...]
    y = o_ref[...] * (g * jax.nn.sigmoid(g))
    r = jnp.dot(y.astype(BF16), wo_ref[...], preferred_element_type=F32)
    out_ref[...] = x_ref[...] + (r * _rms_scale(r)) * gp_ref[...]


def _sb_post(o, g, x, w_out, g_post, block_rows):
    m, d = x.shape
    tm = min(block_rows, m)
    row = pl.BlockSpec((tm, d), lambda i: (i, 0))
    return pl.pallas_call(
        _sb_post_kernel,
        grid=(m // tm,),
        in_specs=[row, row, row, pl.BlockSpec((d, d), lambda i: (0, 0)), pl.BlockSpec((1, d), lambda i: (0, 0))],
        out_specs=row,
        out_shape=jax.ShapeDtypeStruct((m, d), F32),
        compiler_params=_params(("parallel",)),
        name="sb_post",
    )(o, g, x, w_out, g_post.reshape(1, d))


def _suffix_mask(n):
    return (lax.broadcasted_iota(jnp.int32, (n, n), 0) > lax.broadcasted_iota(jnp.int32, (n, n), 1)).astype(BF16)


def _log_fail(z, mask):
    softplus = jnp.maximum(z, 0.0) + jnp.log(1.0 + jnp.exp(-jnp.abs(z)))
    return -softplus if mask is None else jnp.where(mask, -softplus, 0.0)


def _later_in_block(log_fail, tri):
    hi = log_fail.astype(BF16)
    lo = (log_fail - hi.astype(F32)).astype(BF16)
    return jnp.dot(hi, tri, preferred_element_type=F32) + jnp.dot(lo, tri, preferred_element_type=F32)


def _stick_weights(z, log_fail, later, mask):
    w = jnp.exp(z + log_fail + later)
    return w if mask is None else jnp.where(mask, w, 0.0)


def _stick_break_block(z, mask, later_blocks, tri):
    log_fail = _log_fail(z, mask)
    later = _later_in_block(log_fail, tri) + later_blocks
    return _stick_weights(z, log_fail, later, mask), jnp.sum(log_fail, axis=1, keepdims=True)


def _attn_prompt_kernel(bias_ref, q_ref, kt_ref, vt_ref, o_ref, acc_ref, later_ref):
    tb = ATT_BLOCK
    hp = pl.program_id(1)
    qi = pl.program_id(2)
    q = q_ref[0]
    lane = lax.broadcasted_iota(jnp.int32, (1, LANES), 1)
    tri = _suffix_mask(tb)
    diag = lax.broadcasted_iota(jnp.int32, (tb, tb), 1) < lax.broadcasted_iota(jnp.int32, (tb, tb), 0)
    heads = range(HEADS_PER_LANE_TILE)
    qm = [jnp.where((lane >= hh * HEAD_DIM) & (lane < (hh + 1) * HEAD_DIM), q, jnp.zeros_like(q)) for hh in heads]
    bias = [bias_ref[hp * HEADS_PER_LANE_TILE + hh] for hh in heads]
    acc_ref[...] = jnp.zeros_like(acc_ref)
    later_ref[...] = jnp.zeros_like(later_ref)

    def visit(kj, subs_masks):
        kb = kt_ref[0, 0, kj]
        vb = vt_ref[0, 0, kj]
        chains = [(hh, slice(sub * tb, (sub + 1) * tb), mask) for hh in heads for sub, mask in subs_masks]
        zs = [jnp.dot(qm[hh][rows], kb, preferred_element_type=F32) + bias[hh] for hh, rows, _ in chains]
        lfs = [_log_fail(z, mask) for z, (_, _, mask) in zip(zs, chains)]
        laters = [_later_in_block(lf, tri) for lf in lfs]
        ws = [_stick_weights(z, lf, later, mask).astype(BF16)
              for z, lf, later, (_, _, mask) in zip(zs, lfs, laters, chains)]
        for w, lf, later, (hh, rows, _) in zip(ws, lfs, laters, chains):
            c = later_ref[hh, rows]
            pv = lax.dot_general(w, vb, (((1,), (1,)), ((), ())), preferred_element_type=F32)
            acc_ref[hh, rows] += jnp.exp(c) * pv
            later_ref[hh, rows] = c + later[:, 0:1] + lf[:, 0:1]

    first = qi * Q_SUBS
    for lead in reversed(range(Q_SUBS)):
        visit(first + lead, [(lead, diag)] + [(sub, None) for sub in range(lead + 1, Q_SUBS)])

    def body(t, carry):
        visit(first - 1 - t, [(sub, None) for sub in range(Q_SUBS)])
        return carry

    lax.fori_loop(0, first, body, 0)
    o_ref[0] = jnp.where(lane < HEAD_DIM, acc_ref[0], acc_ref[1])


def _attn_prompt(q, kt, vt, bias):
    b, s, hd = q.shape
    n_kb = kt.shape[2]
    tq = Q_SUBS * ATT_BLOCK
    qspec = pl.BlockSpec((1, tq, LANES), lambda i, h, j: (i, j, h))
    kspec = pl.BlockSpec((1, 1, n_kb, LANES, ATT_BLOCK), lambda i, h, j: (i, h, 0, 0, 0))
    return pl.pallas_call(
        _attn_prompt_kernel,
        grid=(b, hd // LANES, s // tq),
        in_specs=[pl.BlockSpec(memory_space=pltpu.SMEM), qspec, kspec, kspec],
        out_specs=qspec,
        out_shape=jax.ShapeDtypeStruct((b, s, hd), F32),
        scratch_shapes=[pltpu.VMEM((HEADS_PER_LANE_TILE, tq, LANES), F32),
                        pltpu.VMEM((HEADS_PER_LANE_TILE, tq, 1), F32)],
        compiler_params=_params(("parallel", "parallel", "arbitrary")),
        name="attn_prompt",
    )(bias, q, kt, vt)


def _attn_paged_kernel(pt_ref, q_ref, knew_ref, vnew_ref, *rest, pages_per_step):
    del pt_ref
    kpage_refs = rest[:pages_per_step]
    vpage_refs = rest[pages_per_step:2 * pages_per_step]
    bias_ref, o_ref, qbd_ref, acc_ref, later_ref = rest[2 * pages_per_step:]
    j = pl.program_id(1)
    nq = q_ref.shape[0]
    hd = q_ref.shape[1]
    n_heads = hd // HEAD_DIM
    rows = n_heads * nq
    tk = knew_ref.shape[1]
    row = lax.broadcasted_iota(jnp.int32, (rows, 1), 0)
    tri = _suffix_mask(tk)
    contract_last = (((1,), (1,)), ((), ()))

    @pl.when(j == 0)
    def _():
        lane = lax.broadcasted_iota(jnp.int32, (1, hd), 1)
        q_rep = jnp.concatenate([q_ref[...]] * n_heads, axis=0)
        qbd = jnp.where(lane // HEAD_DIM == row // nq, q_rep, 0.0).astype(BF16)
        qbd_ref[...] = qbd
        col = lax.broadcasted_iota(jnp.int32, (1, tk), 1)
        z = lax.dot_general(qbd, knew_ref[0].astype(BF16), contract_last, preferred_element_type=F32) + bias_ref[...]
        w, tot = _stick_break_block(z, col < row % nq, 0.0, tri)
        acc_ref[...] = jnp.dot(w.astype(BF16), vnew_ref[0].astype(BF16), preferred_element_type=F32)
        later_ref[...] = tot

    @pl.when(j > 0)
    def _():
        order = list(reversed(range(pages_per_step)))
        zs = [jnp.dot(qbd_ref[...], kpage_refs[c][0].reshape(hd, tk).astype(BF16), preferred_element_type=F32)
              + bias_ref[...] for c in order]
        lfs = [_log_fail(z, None) for z in zs]
        laters = [_later_in_block(lf, tri) for lf in lfs]
        later_blocks = later_ref[...]
        for c, z, lf, later in zip(order, zs, lfs, laters):
            w = _stick_weights(z, lf, later + later_blocks, None).astype(BF16)
            acc_ref[...] += lax.dot_general(w, vpage_refs[c][0].reshape(hd, tk).astype(BF16), contract_last,
                                            preferred_element_type=F32)
            later_blocks = later_blocks + later[:, 0:1] + lf[:, 0:1]
        later_ref[...] = later_blocks

    @pl.when(j == pl.num_programs(1) - 1)
    def _():
        lane = lax.broadcasted_iota(jnp.int32, (1, hd), 1)
        out = jnp.zeros((nq, hd), F32)
        for h in range(n_heads):
            out = out + jnp.where(lane // HEAD_DIM == h, acc_ref[h * nq:(h + 1) * nq, :], 0.0)
        o_ref[...] = out


def _attn_paged(q, k_new, v_new, cache_kt, cache_vt, page_table, bias, n_new):
    n_seq, n_pages = page_table.shape
    hd = q.shape[1]
    n_heads = hd // HEAD_DIM
    rows = n_heads * n_new
    page = cache_kt.shape[3]
    pps = PAGES_PER_STEP
    seq_rows = pl.BlockSpec((n_new, hd), lambda n, j, pt: (n, 0))
    new_spec = pl.BlockSpec((1, page, hd), lambda n, j, pt: (n, 0, 0))

    def page_spec(c):
        return pl.BlockSpec((1, n_heads, HEAD_DIM, page),
                            lambda n, j, pt: (pt[n, n_pages - jnp.maximum(j, 1) * pps + c], 0, 0, 0))

    grid_spec = pltpu.PrefetchScalarGridSpec(
        num_scalar_prefetch=1,
        grid=(n_seq, n_pages // pps + 1),
        in_specs=([seq_rows, new_spec, new_spec] + [page_spec(c) for c in range(pps)] * 2
                  + [pl.BlockSpec((rows, 1), lambda n, j, pt: (0, 0))]),
        out_specs=seq_rows,
        scratch_shapes=[pltpu.VMEM((rows, hd), BF16), pltpu.VMEM((rows, hd), F32), pltpu.VMEM((rows, 1), F32)],
    )
    return pl.pallas_call(
        functools.partial(_attn_paged_kernel, pages_per_step=pps),
        grid_spec=grid_spec,
        out_shape=jax.ShapeDtypeStruct((n_seq * n_new, hd), F32),
        compiler_params=_params(("parallel", "arbitrary")),
        name="attn_paged",
    )(page_table, q, k_new, v_new, *([cache_kt] * pps), *([cache_vt] * pps),
      jnp.repeat(bias, n_new).reshape(rows, 1))


def _s5_layer(xp, xs, n_prompt, h0_re, h0_im, prm):
    (g_pre, g_post, w_in, log_dt, a_re, a_im, b_re, b_im, c_re, c_im, d_skip, w_glu, b_glu, w_out) = prm
    d = xp.shape[1]
    n_groups, n_state = a_re.shape
    gc = b_re.shape[-1]
    width = n_groups * gc
    w1, w2, coef = _ssm_prep(log_dt, a_re, a_im, b_re, b_im, c_re, c_im)
    w_in_b, w_glu_b, w_out_b = w_in.astype(BF16), w_glu.astype(BF16), w_out.astype(BF16)
    uz_outs = [(0, width, 1.0, F32), (width, 2 * width, 1.0, F32)]

    u, z = _norm_matmul(xp, g_pre, w_in_b, uz_outs, 512)
    seq = xp.shape[0] // n_prompt
    nk = seq // SSM_CHUNK
    u2 = (u.reshape(n_prompt, nk, SSM_CHUNK, n_groups, gc).transpose(0, 3, 1, 2, 4)
          .reshape(n_prompt, n_groups, nk, SSM_CHUNK * gc).astype(BF16))
    y2, hfin = _ssm_seq(u2, w1, w2, coef)
    y = (y2.reshape(n_prompt, n_groups, nk, SSM_CHUNK, gc).transpose(0, 2, 3, 1, 4).reshape(xp.shape[0], width))
    xp = _s5_post(y, u, z, xp, d_skip, w_glu_b, b_glu, w_out_b, g_post, 512)

    n_seq = h0_re.shape[0]
    t_new = xs.shape[0] // n_seq
    us, zs = _norm_matmul(xs, g_pre, w_in_b, uz_outs, 512)
    u8 = (us.reshape(n_seq, t_new, n_groups, gc).transpose(2, 0, 1, 3)
          .reshape(n_groups, n_seq, t_new * gc).astype(BF16))
    h0 = jnp.concatenate([h0_re, h0_im], axis=-1).transpose(1, 0, 2)
    h0s = jnp.concatenate([h0_im, h0_re], axis=-1).transpose(1, 0, 2)
    y8, hf8 = _ssm_step(u8, h0, h0s, w1, w2, coef)
    ys = y8.reshape(n_groups, n_seq, t_new, gc).transpose(1, 2, 0, 3).reshape(xs.shape[0], width)
    xs = _s5_post(ys, us, zs, xs, d_skip, w_glu_b, b_glu, w_out_b, g_post, 512)
    hf8 = hf8.transpose(1, 0, 2)
    return (xp, xs, hfin[..., :n_state], hfin[..., n_state:], hf8[..., :n_state], hf8[..., n_state:])


def kernel(x_prompt, x_sample, state_ssm_re, state_ssm_im, cache_k, cache_v, page_table, a_norm_pre, a_norm_post, a_w_in, a_log_dt, a_A_re, a_A_im, a_B_re, a_B_im, a_C_re, a_C_im, a_D, a_w_glu, a_b_glu, a_w_out, kv_norm, w_kv, b_norm_pre, b_norm_post, b_w_in, b_logit_bias, b_w_out):
    n_prompt, seq, d = x_prompt.shape
    n_seq, t_new, _ = x_sample.shape
    n_a = a_w_in.shape[0]
    n_b = b_w_in.shape[0]
    att = w_kv.shape[1] // 2
    n_heads = att // HEAD_DIM
    assert t_new * 2 == SSM_CHUNK and cache_k.shape[1] == PAGE_SIZE and HEAD_DIM ** -0.5 == 0.125

    xp = x_prompt.reshape(n_prompt * seq, d)
    xs = x_sample.reshape(n_seq * t_new, d)
    p_re, p_im, s_re, s_im = [], [], [], []
    for i in range(n_a):
        prm = (a_norm_pre[i], a_norm_post[i], a_w_in[i], a_log_dt[i], a_A_re[i], a_A_im[i], a_B_re[i], a_B_im[i],
               a_C_re[i], a_C_im[i], a_D[i], a_w_glu[i], a_b_glu[i], a_w_out[i])
        xp, xs, hr, hi, sr, si = _s5_layer(xp, xs, n_prompt, state_ssm_re[i], state_ssm_im[i], prm)
        p_re.append(hr)
        p_im.append(hi)
        s_re.append(sr)
        s_im.append(si)

    w_kv_b = w_kv.astype(BF16)
    kt_p, vt_p, kt_pb, vt_pb = _kv_proj(xp, kv_norm, w_kv_b.T, n_prompt, 512)
    k_p = kt_p.reshape(n_prompt, n_heads, HEAD_DIM, seq).transpose(0, 3, 1, 2)
    v_p = vt_p.reshape(n_prompt, n_heads, HEAD_DIM, seq).transpose(0, 3, 1, 2)
    k_s, v_s = _norm_matmul(xs, kv_norm, w_kv_b, [(0, att, 1.0, F32), (att, 2 * att, 1.0, F32)], 512)
    pad_new = lambda a: jnp.pad(a.reshape(n_seq, t_new, att), ((0, 0), (0, PAGE_SIZE - t_new), (0, 0)))
    k_s_pad, v_s_pad = pad_new(k_s), pad_new(v_s)
    cache_kt = cache_k.transpose(0, 2, 3, 1)
    cache_vt = cache_v.transpose(0, 2, 3, 1)

    scale = HEAD_DIM ** -0.5
    for j in range(n_b):
        w_in_b, w_out_b = b_w_in[j].astype(BF16), b_w_out[j].astype(BF16)
        q, g = _norm_matmul(xp, b_norm_pre[j], w_in_b, [(0, att, scale, BF16), (att, 2 * att, 1.0, F32)], 512)
        o = _attn_prompt(q.reshape(n_prompt, seq, att), kt_pb, vt_pb, b_logit_bias[j])
        xp = _sb_post(o.reshape(n_prompt * seq, att), g, xp, w_out_b, b_norm_post[j], 512)
        qs, gs = _norm_matmul(xs, b_norm_pre[j], w_in_b, [(0, att, scale, F32), (att, 2 * att, 1.0, F32)], 512)
        os_ = _attn_paged(qs, k_s_pad, v_s_pad, cache_kt, cache_vt, page_table, b_logit_bias[j], t_new)
        xs = _sb_post(os_, gs, xs, w_out_b, b_norm_post[j], 512)

    return (xp.reshape(n_prompt, seq, d), xs.reshape(n_seq, t_new, d),
            jnp.stack(p_re), jnp.stack(p_im),
            k_p, v_p,
            jnp.stack(s_re), jnp.stack(s_im),
            k_s.reshape(n_seq, t_new, n_heads, HEAD_DIM), v_s.reshape(n_seq, t_new, n_heads, HEAD_DIM))
```

```python
import functools

import jax
import jax.numpy as jnp
from jax import lax
from jax.experimental import pallas as pl
from jax.experimental.pallas import tpu as pltpu

F32 = jnp.float32
BF16 = jnp.bfloat16

RMS_EPS = 1e-6
SSM_GROUP = 16
SSM_STATE = 64
SSM_CHUNK = 16
SSM_GROUP_BLOCK = 8
LAM_POWERS = 5
TAB_ROWS = 16
HEAD_DIM = 64
LANES = 128
HEADS_PER_LANE_TILE = LANES // HEAD_DIM
ATT_BLOCK = 256
Q_SUBS = 4
BIAS_TERMS = 3
PAGE_SIZE = 128
PAGES_PER_STEP = 4
VMEM_LIMIT = 48 * 1024 * 1024


def _params(semantics):
    return pltpu.CompilerParams(dimension_semantics=semantics, vmem_limit_bytes=VMEM_LIMIT)


def _rms_scale(x):
    return lax.rsqrt(jnp.mean(x * x, axis=-1, keepdims=True) + RMS_EPS)


def _norm_matmul_kernel(x_ref, g_ref, w_ref, *o_refs, splits):
    x = x_ref[...]
    h = (x * _rms_scale(x)) * g_ref[...]
    acc = jnp.dot(h.astype(BF16), w_ref[...], preferred_element_type=F32)
    for o_ref, (lo, hi, scale) in zip(o_refs, splits):
        v = acc[:, lo:hi]
        if scale != 1.0:
            v = v * scale
        o_ref[...] = v.astype(o_ref.dtype)


def _norm_matmul(x, g, w, outs, block_rows):
    m, d = x.shape
    n = w.shape[1]
    tm = min(block_rows, m)
    splits = tuple((lo, hi, scale) for lo, hi, scale, _ in outs)
    return pl.pallas_call(
        functools.partial(_norm_matmul_kernel, splits=splits),
        grid=(m // tm,),
        in_specs=[pl.BlockSpec((tm, d), lambda i: (i, 0)),
                  pl.BlockSpec((1, d), lambda i: (0, 0)),
                  pl.BlockSpec((d, n), lambda i: (0, 0))],
        out_specs=[pl.BlockSpec((tm, hi - lo), lambda i: (i, 0)) for lo, hi, _, _ in outs],
        out_shape=[jax.ShapeDtypeStruct((m, hi - lo), dt) for lo, hi, _, dt in outs],
        compiler_params=_params(("parallel",)),
        name="norm_matmul",
    )(x, g.reshape(1, d), w)


def _kv_proj_kernel(x_ref, g_ref, wt_ref, kt_ref, vt_ref, ktb_ref, vtb_ref):
    x = x_ref[...]
    h = (x * _rms_scale(x)) * g_ref[...]
    acc = lax.dot_general(wt_ref[...], h.astype(BF16), (((1,), (1,)), ((), ())), preferred_element_type=F32)
    att = kt_ref.shape[1]
    n_blocks = ktb_ref.shape[2]
    for t_ref, b_ref, lo in ((kt_ref, ktb_ref, 0), (vt_ref, vtb_ref, att)):
        part = acc[lo:lo + att]
        t_ref[0] = part
        for c in range(n_blocks):
            blk = part[:, c * ATT_BLOCK:(c + 1) * ATT_BLOCK]
            b_ref[0, :, c] = blk.reshape(att // LANES, LANES, ATT_BLOCK).astype(b_ref.dtype)


def _kv_proj(x, g, wt, n_prompt, block_rows):
    m, d = x.shape
    att = wt.shape[0] // 2
    seq = m // n_prompt
    tm = min(block_rows, seq)
    steps = seq // tm
    n_blocks = tm // ATT_BLOCK
    t_spec = pl.BlockSpec((1, att, tm), lambda b, i: (b, 0, i))
    b_spec = pl.BlockSpec((1, att // LANES, n_blocks, LANES, ATT_BLOCK), lambda b, i: (b, 0, i, 0, 0))
    blocked = jax.ShapeDtypeStruct((n_prompt, att // LANES, seq // ATT_BLOCK, LANES, ATT_BLOCK), BF16)
    return pl.pallas_call(
        _kv_proj_kernel,
        grid=(n_prompt, steps),
        in_specs=[pl.BlockSpec((tm, d), lambda b, i: (b * steps + i, 0)),
                  pl.BlockSpec((1, d), lambda b, i: (0, 0)),
                  pl.BlockSpec((2 * att, d), lambda b, i: (0, 0))],
        out_specs=[t_spec, t_spec, b_spec, b_spec],
        out_shape=[jax.ShapeDtypeStruct((n_prompt, att, seq), F32)] * 2 + [blocked] * 2,
        compiler_params=_params(("parallel", "parallel")),
        name="kv_proj",
    )(x, g.reshape(1, d), wt)


def _ssm_lam_kernel(log_dt_ref, a_re_ref, a_im_ref, tab_ref):
    a_re = a_re_ref[...]
    a_im = a_im_ref[...]
    dt = jnp.exp(log_dt_ref[...])
    mag = jnp.exp(dt * a_re)
    lr = mag * jnp.cos(dt * a_im)
    li = mag * jnp.sin(dt * a_im)
    den = a_re * a_re + a_im * a_im
    tab_ref[0] = ((lr - 1.0) * a_re + li * a_im) / den
    tab_ref[1] = (li * a_re - (lr - 1.0) * a_im) / den
    for b in range(LAM_POWERS):
        tab_ref[2 + 2 * b] = lr
        tab_ref[3 + 2 * b] = li
        lr, li = lr * lr - li * li, 2.0 * (lr * li)


def _lam_pow(tau, pow_re, pow_im, shape):
    pr = pi = None
    for b in range(LAM_POWERS - 1):
        on = ((tau >> b) & 1) == 1
        fr = jnp.where(on, jnp.broadcast_to(pow_re[b], shape), 1.0)
        fi = jnp.where(on, jnp.broadcast_to(pow_im[b], shape), 0.0)
        pr, pi = (fr, fi) if pr is None else (pr * fr - pi * fi, pr * fi + pi * fr)
    return pr, pi


def _ssm_prep_kernel(colp_ref, rowp_ref, ct_re_ref, ct_im_ref, bt_re_ref, bt_im_ref,
                     w1_ref, w2_ref, coef_ref):
    n_state2 = 2 * SSM_STATE
    n_in = SSM_CHUNK * SSM_GROUP
    col = lambda i: colp_ref[0, :, i:i + 1]
    tau_l = lax.broadcasted_iota(jnp.int32, (1, n_in), 1) // SSM_GROUP
    row_is_re = lax.broadcasted_iota(jnp.int32, (n_state2, 1), 0) < SSM_STATE
    ct_re = ct_re_ref[0]
    ct_im = ct_im_ref[0]
    pr, pi = _lam_pow(tau_l, [col(2 + 2 * b) for b in range(LAM_POWERS)],
                      [col(3 + 2 * b) for b in range(LAM_POWERS)], (n_state2, n_in))
    g_re, g_im = ct_re * pr - ct_im * pi, ct_re * pi + ct_im * pr
    rhs_k = jnp.where(row_is_re, g_re, g_im)
    q_re, q_im = g_re * col(2) - g_im * col(3), g_re * col(3) + g_im * col(2)
    w2_ref[0] = jnp.where(row_is_re, q_re, -q_im).astype(w2_ref.dtype)

    row = lambda i: rowp_ref[0, i:i + 1, :]
    lane_is_re = lax.broadcasted_iota(jnp.int32, (1, n_state2), 1) < SSM_STATE
    w_re, w_im = row(0), row(1)
    bt_re = bt_re_ref[0]
    bt_im = bt_im_ref[0]
    bb_re = w_re * bt_re - w_im * bt_im
    bb_im = w_re * bt_im + w_im * bt_re

    lhs_k = jnp.where(lane_is_re, bb_re[0:SSM_GROUP], -bb_im[0:SSM_GROUP])
    kk = jnp.dot(lhs_k, rhs_k, preferred_element_type=F32, precision=lax.Precision.HIGHEST)
    lane_in = lax.broadcasted_iota(jnp.int32, (1, n_in), 1)
    for s in range(SSM_CHUNK):
        shifted = kk if s == 0 else pltpu.roll(kk, s * SSM_GROUP, 1)
        piece = jnp.where(lane_in >= s * SSM_GROUP, shifted, 0.0)
        w1_ref[0, s * SSM_GROUP:(s + 1) * SSM_GROUP, 0:n_in] = piece.astype(w1_ref.dtype)

    tau_row = SSM_CHUNK - 1 - lax.broadcasted_iota(jnp.int32, (n_in, 1), 0) // SSM_GROUP
    pr_r, pi_r = _lam_pow(tau_row, [row(2 + 2 * b) for b in range(LAM_POWERS)],
                          [row(3 + 2 * b) for b in range(LAM_POWERS)], (n_in, n_state2))
    p_re = bb_re * pr_r - bb_im * pi_r
    p_im = bb_re * pi_r + bb_im * pr_r
    w1_ref[0, :, n_in:n_in + n_state2] = jnp.where(lane_is_re, p_re, p_im).astype(w1_ref.dtype)
    w1_ref[0, :, n_in + n_state2:n_in + 2 * n_state2] = jnp.where(lane_is_re, p_im, p_re).astype(w1_ref.dtype)

    full, half = 2 * (LAM_POWERS - 1), 2 * (LAM_POWERS - 2)
    zero = jnp.zeros_like(w_re)
    coef_ref[0] = jnp.concatenate(
        [row(2 + full), jnp.where(lane_is_re, -row(3 + full), row(3 + full)),
         row(2 + half), jnp.where(lane_is_re, -row(3 + half), row(3 + half)), zero, zero, zero, zero], axis=0)


def _ssm_prep(log_dt, a_re, a_im, b_re, b_im, c_re, c_im):
    g, p = a_re.shape
    c = b_re.shape[-1]
    n_in = SSM_CHUNK * c
    n_tab = 2 + 2 * LAM_POWERS
    tab = pl.pallas_call(
        _ssm_lam_kernel,
        out_shape=jax.ShapeDtypeStruct((n_tab, g, p), F32),
        name="ssm_lam",
    )(log_dt.reshape(g, 1), a_re, a_im)
    tab2 = jnp.concatenate([tab, tab], axis=-1)
    pad = TAB_ROWS - n_tab
    cols = jnp.pad(tab2.transpose(1, 2, 0), ((0, 0), (0, 0), (0, pad)))
    rows = jnp.pad(tab2.transpose(1, 0, 2), ((0, 0), (0, pad), (0, 0)))
    ct = lambda a: jnp.tile(jnp.swapaxes(a, 1, 2), (1, 2, SSM_CHUNK))
    bt = lambda a: jnp.tile(jnp.swapaxes(a, 1, 2), (1, SSM_CHUNK, 2))
    spec3 = lambda s1, s2: pl.BlockSpec((1, s1, s2), lambda i: (i, 0, 0))
    w1, w2, coef = pl.pallas_call(
        _ssm_prep_kernel,
        grid=(g,),
        in_specs=[spec3(2 * p, TAB_ROWS), spec3(TAB_ROWS, 2 * p), spec3(2 * p, n_in), spec3(2 * p, n_in),
                  spec3(n_in, 2 * p), spec3(n_in, 2 * p)],
        out_specs=[spec3(n_in, n_in + 4 * p), spec3(2 * p, n_in), spec3(8, 2 * p)],
        out_shape=[jax.ShapeDtypeStruct((g, n_in, n_in + 4 * p), BF16),
                   jax.ShapeDtypeStruct((g, 2 * p, n_in), BF16),
                   jax.ShapeDtypeStruct((g, 8, 2 * p), F32)],
        compiler_params=_params(("parallel",)),
        name="ssm_prep",
    )(cols, rows, ct(c_re), ct(c_im), bt(b_re), bt(b_im))
    return w1, w2, jnp.swapaxes(coef, 0, 1)[:4]


def _swap_array_and_lane_block(xs, lane_block_width):
    n = len(xs)
    lanes = n * lane_block_width
    blk = lax.broadcasted_iota(jnp.int32, (1, lanes), 1) // lane_block_width
    d = n // 2
    while d >= 1:
        keep = (blk & d) == 0
        new = list(xs)
        for i in range(n):
            if i & d == 0:
                lo, hi = xs[i], xs[i + d]
                new[i] = jnp.where(keep, lo, pltpu.roll(hi, d * lane_block_width, 1))
                new[i + d] = jnp.where(keep, pltpu.roll(lo, lanes - d * lane_block_width, 1), hi)
        xs = new
        d //= 2
    return xs


def _ssm_seq_kernel(u_ref, w1_ref, w2_ref, coef_ref, y_ref, hfin_ref, s_ref, ss_ref, yc_ref):
    gb = SSM_GROUP_BLOCK
    nk = u_ref.shape[0] // SSM_CHUNK
    n_in = SSM_CHUNK * SSM_GROUP
    n_state2 = 2 * SSM_STATE
    steps_per_tile = LANES // SSM_GROUP
    n_tiles = SSM_CHUNK // steps_per_tile
    assert gb * SSM_GROUP == LANES and steps_per_tile == gb

    u_cm = []
    for t in range(n_tiles):
        by_step = [u_ref[pl.ds(t * steps_per_tile + i, nk, stride=SSM_CHUNK), :] for i in range(steps_per_tile)]
        u_cm.append(_swap_array_and_lane_block(by_step, SSM_GROUP))
    for g in range(gb):
        u_g = jnp.concatenate([u_cm[t][g] for t in range(n_tiles)], axis=1).astype(BF16)
        r = jnp.dot(u_g, w1_ref[g], preferred_element_type=F32)
        yc_ref[g] = r[:, 0:n_in]
        s_ref[pl.ds(g, nk, stride=gb), :] = r[:, n_in:n_in + n_state2]
        ss_ref[pl.ds(g, nk, stride=gb), :] = r[:, n_in + n_state2:n_in + 2 * n_state2]
    a = coef_ref[0]
    b = coef_ref[1]

    def step(k, carry):
        x, xs = carry
        r0 = pl.multiple_of(k * gb, gb)
        add = s_ref[pl.ds(r0, gb), :]
        adds = ss_ref[pl.ds(r0, gb), :]
        s_ref[pl.ds(r0, gb), :] = x
        return a * x + b * xs + add, a * xs - b * x + adds

    zero = jnp.zeros((gb, n_state2), F32)
    x_fin, _ = lax.fori_loop(0, nk, step, (zero, zero))
    hfin_ref[0] = x_fin
    for g in range(gb):
        h_prev = s_ref[pl.ds(g, nk, stride=gb), :]
        yc_ref[g] += jnp.dot(h_prev.astype(BF16), w2_ref[g], preferred_element_type=F32)
    for t in range(n_tiles):
        by_group = [yc_ref[g, :, t * LANES:(t + 1) * LANES] for g in range(gb)]
        by_step = _swap_array_and_lane_block(by_group, SSM_GROUP)
        for i in range(steps_per_tile):
            y_ref[pl.ds(t * steps_per_tile + i, nk, stride=SSM_CHUNK), :] = by_step[i]


def _ssm_seq(u, w1, w2, coef, n_seq):
    m, width = u.shape
    seq = m // n_seq
    g = w1.shape[0]
    gb = SSM_GROUP_BLOCK
    nk = seq // SSM_CHUNK
    n_in = SSM_CHUNK * SSM_GROUP
    n_state2 = 2 * SSM_STATE
    tok = pl.BlockSpec((seq, LANES), lambda i, j: (i, j))
    return pl.pallas_call(
        _ssm_seq_kernel,
        grid=(n_seq, g // gb),
        in_specs=[tok,
                  pl.BlockSpec((gb, n_in, n_in + 2 * n_state2), lambda i, j: (j, 0, 0)),
                  pl.BlockSpec((gb, n_state2, n_in), lambda i, j: (j, 0, 0)),
                  pl.BlockSpec((4, gb, n_state2), lambda i, j: (0, j, 0))],
        out_specs=[tok, pl.BlockSpec((1, gb, n_state2), lambda i, j: (i, j, 0))],
        out_shape=[jax.ShapeDtypeStruct((m, width), F32),
                   jax.ShapeDtypeStruct((n_seq, g, n_state2), F32)],
        scratch_shapes=[pltpu.VMEM((nk * gb, n_state2), F32), pltpu.VMEM((nk * gb, n_state2), F32),
                        pltpu.VMEM((gb, nk, n_in), F32)],
        compiler_params=_params(("parallel", "parallel")),
        name="ssm_seq",
    )(u, w1, w2, coef)


def _ssm_step_kernel(u_ref, h0_ref, h0s_ref, w1_ref, w2_ref, coef_ref, y_ref, hfin_ref):
    gb = u_ref.shape[0]
    n_half = u_ref.shape[2]
    n_in = 2 * n_half
    n_state2 = 2 * SSM_STATE
    for g in range(gb):
        u = u_ref[g]
        h0 = h0_ref[g]
        t_half = w1_ref[g, 0:n_half, 0:n_half]
        p_half = w1_ref[g, n_half:n_in, n_in:n_in + n_state2]
        q_half = w2_ref[g, :, 0:n_half]
        y_ref[g] = (jnp.dot(u, t_half, preferred_element_type=F32)
                    + jnp.dot(h0.astype(BF16), q_half, preferred_element_type=F32))
        hfin_ref[g] = (coef_ref[2, g:g + 1, :] * h0 + coef_ref[3, g:g + 1, :] * h0s_ref[g]
                       + jnp.dot(u, p_half, preferred_element_type=F32))


def _ssm_step(u8, h0, h0s, w1, w2, coef):
    g, n, n_half = u8.shape
    gb = SSM_GROUP_BLOCK
    n_state2 = 2 * SSM_STATE
    n_in = 2 * n_half
    blk = lambda s1, s2: pl.BlockSpec((gb, s1, s2), lambda j: (j, 0, 0))
    return pl.pallas_call(
        _ssm_step_kernel,
        grid=(g // gb,),
        in_specs=[blk(n, n_half), blk(n, n_state2), blk(n, n_state2),
                  blk(n_in, n_in + 2 * n_state2), blk(n_state2, n_in),
                  pl.BlockSpec((4, gb, n_state2), lambda j: (0, j, 0))],
        out_specs=[blk(n, n_half), blk(n, n_state2)],
        out_shape=[jax.ShapeDtypeStruct((g, n, n_half), F32), jax.ShapeDtypeStruct((g, n, n_state2), F32)],
        compiler_params=_params(("parallel",)),
        name="ssm_step",
    )(u8, h0, h0s, w1, w2, coef)


def _s5_post_kernel(y_ref, u_ref, z_ref, x_ref, d_ref, wg_ref, bg_ref, wo_ref, gp_ref, o_ref):
    y = jax.nn.gelu(y_ref[...] + d_ref[...] * u_ref[...])
    gate = jnp.dot(y.astype(BF16), wg_ref[...], preferred_element_type=F32) + bg_ref[...]
    y = y * jax.nn.sigmoid(gate)
    z = z_ref[...]
    y = y * (z * jax.nn.sigmoid(z))
    o = jnp.dot(y.astype(BF16), wo_ref[...], preferred_element_type=F32)
    o_ref[...] = x_ref[...] + (o * _rms_scale(o)) * gp_ref[...]


def _s5_post(y, u, z, x, d_skip, w_glu, b_glu, w_out, g_post, block_rows):
    m, d = x.shape
    tm = min(block_rows, m)
    row = pl.BlockSpec((tm, d), lambda i: (i, 0))
    vec = pl.BlockSpec((1, d), lambda i: (0, 0))
    mat = pl.BlockSpec((d, d), lambda i: (0, 0))
    return pl.pallas_call(
        _s5_post_kernel,
        grid=(m // tm,),
        in_specs=[row, row, row, row, vec, mat, vec, mat, vec],
        out_specs=row,
        out_shape=jax.ShapeDtypeStruct((m, d), F32),
        compiler_params=_params(("parallel",)),
        name="s5_post",
    )(y, u, z, x, d_skip.reshape(1, d), w_glu, b_glu.reshape(1, d), w_out, g_post.reshape(1, d))


def _sb_post_kernel(o_ref, g_ref, x_ref, wo_ref, gp_ref, out_ref):
    g = g_ref[...]
    y = o_ref[...] * (g * jax.nn.sigmoid(g))
    r = jnp.dot(y.astype(BF16), wo_ref[...], preferred_element_type=F32)
    out_ref[...] = x_ref[...] + (r * _rms_scale(r)) * gp_ref[...]


def _sb_post(o, g, x, w_out, g_post, block_rows):
    m, d = x.shape
    tm = min(block_rows, m)
    row = pl.BlockSpec((tm, d), lambda i: (i, 0))
    return pl.pallas_call(
        _sb_post_kernel,
        grid=(m // tm,),
        in_specs=[row, row, row, pl.BlockSpec((d, d), lambda i: (0, 0)), pl.BlockSpec((1, d), lambda i: (0, 0))],
        out_specs=row,
        out_shape=jax.ShapeDtypeStruct((m, d), F32),
        compiler_params=_params(("parallel",)),
        name="sb_post",
    )(o, g, x, w_out, g_post.reshape(1, d))


def _suffix_mask(n):
    j = lax.broadcasted_iota(jnp.int32, (2 * n, n), 0)
    s = lax.broadcasted_iota(jnp.int32, (2 * n, n), 1)
    return ((j >= s) & ((j < n) | (j >= s + n))).astype(BF16)


def _miss(z, mask):
    sp = jnp.maximum(z, 0.0) + jnp.log(1.0 + jnp.exp(-jnp.abs(z)))
    return sp if mask is None else jnp.where(mask, sp, 0.0)


def _suffix_sums(miss, tri):
    hi = miss.astype(BF16)
    lo = (miss - hi.astype(F32)).astype(BF16)
    return jnp.dot(jnp.concatenate([hi, lo], axis=1), tri, preferred_element_type=F32)


def _stick_weights(z, misses_from_here, mask):
    w = jnp.exp(z - misses_from_here)
    return w if mask is None else jnp.where(mask, w, 0.0)


def _attn_prompt_kernel(bias_ref, q_ref, kt_ref, vt_ref, o_ref, acc_ref, later_ref):
    tb = ATT_BLOCK
    hp = pl.program_id(1)
    qi = pl.program_id(2)
    q = q_ref[0]
    lane = lax.broadcasted_iota(jnp.int32, (1, LANES), 1)
    tri = _suffix_mask(tb)
    diag = lax.broadcasted_iota(jnp.int32, (tb, tb), 1) < lax.broadcasted_iota(jnp.int32, (tb, tb), 0)
    heads = range(HEADS_PER_LANE_TILE)
    q_aug = []
    for hh in heads:
        qm = jnp.where((lane >= hh * HEAD_DIM) & (lane < (hh + 1) * HEAD_DIM), q, jnp.zeros_like(q))
        rest = jnp.full((1, LANES), bias_ref[hp * HEADS_PER_LANE_TILE + hh], F32)
        terms = jnp.zeros((1, LANES), F32)
        for t in range(BIAS_TERMS):
            term = rest.astype(BF16).astype(F32)
            terms = jnp.where(lane == t, term, terms)
            rest = rest - term
        q_aug.append(jnp.concatenate([qm, jnp.broadcast_to(terms, qm.shape).astype(BF16)], axis=1))
    ones_rows = (lax.broadcasted_iota(jnp.int32, (LANES, tb), 0) < BIAS_TERMS).astype(BF16)
    acc_ref[...] = jnp.zeros_like(acc_ref)
    later_ref[...] = jnp.zeros_like(later_ref)

    def visit(kj, subs_masks):
        kb = jnp.concatenate([kt_ref[0, 0, kj], ones_rows], axis=0)
        vb = vt_ref[0, 0, kj]
        chains = [(hh, slice(sub * tb, (sub + 1) * tb), mask) for hh in heads for sub, mask in subs_masks]
        zs = [jnp.dot(q_aug[hh][rows], kb, preferred_element_type=F32) for hh, rows, _ in chains]
        misses = [_miss(z, mask) for z, (_, _, mask) in zip(zs, chains)]
        sums = [_suffix_sums(m, tri) for m in misses]
        ws = [_stick_weights(z, sm, mask).astype(BF16) for z, sm, (_, _, mask) in zip(zs, sums, chains)]
        for w, sm, (hh, rows, _) in zip(ws, sums, chains):
            c = later_ref[hh, rows]
            pv = lax.dot_general(w, vb, (((1,), (1,)), ((), ())), preferred_element_type=F32)
            acc_ref[hh, rows] += jnp.exp(-c) * pv
            later_ref[hh, rows] = c + sm[:, 0:1]

    first = qi * Q_SUBS
    for lead in reversed(range(Q_SUBS)):
        visit(first + lead, [(lead, diag)] + [(sub, None) for sub in range(lead + 1, Q_SUBS)])

    def body(t, carry):
        visit(first - 1 - t, [(sub, None) for sub in range(Q_SUBS)])
        return carry

    lax.fori_loop(0, first, body, 0)
    o_ref[0] = jnp.where(lane < HEAD_DIM, acc_ref[0], acc_ref[1])


def _attn_prompt(q, kt, vt, bias):
    b, s, hd = q.shape
    n_kb = kt.shape[2]
    tq = Q_SUBS * ATT_BLOCK
    qspec = pl.BlockSpec((1, tq, LANES), lambda i, h, j: (i, j, h))
    kspec = pl.BlockSpec((1, 1, n_kb, LANES, ATT_BLOCK), lambda i, h, j: (i, h, 0, 0, 0))
    return pl.pallas_call(
        _attn_prompt_kernel,
        grid=(b, hd // LANES, s // tq),
        in_specs=[pl.BlockSpec(memory_space=pltpu.SMEM), qspec, kspec, kspec],
        out_specs=qspec,
        out_shape=jax.ShapeDtypeStruct((b, s, hd), F32),
        scratch_shapes=[pltpu.VMEM((HEADS_PER_LANE_TILE, tq, LANES), F32),
                        pltpu.VMEM((HEADS_PER_LANE_TILE, tq, 1), F32)],
        compiler_params=_params(("parallel", "parallel", "arbitrary")),
        name="attn_prompt",
    )(bias, q, kt, vt)


def _attn_paged_kernel(pt_ref, q_ref, knew_ref, vnew_ref, *rest, pages_per_step):
    del pt_ref
    kpage_refs = rest[:pages_per_step]
    vpage_refs = rest[pages_per_step:2 * pages_per_step]
    bias_ref, o_ref, qbd_ref, acc_ref, later_ref = rest[2 * pages_per_step:]
    j = pl.program_id(1)
    nq = q_ref.shape[0]
    hd = q_ref.shape[1]
    n_heads = hd // HEAD_DIM
    rows = n_heads * nq
    tk = knew_ref.shape[1]
    row = lax.broadcasted_iota(jnp.int32, (rows, 1), 0)
    tri = _suffix_mask(tk)
    contract_last = (((1,), (1,)), ((), ()))

    @pl.when(j == 0)
    def _():
        lane = lax.broadcasted_iota(jnp.int32, (1, hd), 1)
        q_rep = jnp.concatenate([q_ref[...]] * n_heads, axis=0)
        qbd = jnp.where(lane // HEAD_DIM == row // nq, q_rep, 0.0).astype(BF16)
        qbd_ref[...] = qbd
        col = lax.broadcasted_iota(jnp.int32, (1, tk), 1)
        mask = col < row % nq
        z = lax.dot_general(qbd, knew_ref[0].astype(BF16), contract_last, preferred_element_type=F32) + bias_ref[...]
        sums = _suffix_sums(_miss(z, mask), tri)
        w = _stick_weights(z, sums, mask).astype(BF16)
        acc_ref[...] = jnp.dot(w, vnew_ref[0].astype(BF16), preferred_element_type=F32)
        later_ref[...] = sums[:, 0:1]

    @pl.when(j > 0)
    def _():
        order = list(reversed(range(pages_per_step)))
        zs = [jnp.dot(qbd_ref[...], kpage_refs[c][0].reshape(hd, tk).astype(BF16), preferred_element_type=F32)
              + bias_ref[...] for c in order]
        sums = [_suffix_sums(_miss(z, None), tri) for z in zs]
        later_blocks = later_ref[...]
        for c, z, sm in zip(order, zs, sums):
            w = _stick_weights(z, sm + later_blocks, None).astype(BF16)
            acc_ref[...] += lax.dot_general(w, vpage_refs[c][0].reshape(hd, tk).astype(BF16), contract_last,
                                            preferred_element_type=F32)
            later_blocks = later_blocks + sm[:, 0:1]
        later_ref[...] = later_blocks

    @pl.when(j == pl.num_programs(1) - 1)
    def _():
        lane = lax.broadcasted_iota(jnp.int32, (1, hd), 1)
        out = jnp.zeros((nq, hd), F32)
        for h in range(n_heads):
            out = out + jnp.where(lane // HEAD_DIM == h, acc_ref[h * nq:(h + 1) * nq, :], 0.0)
        o_ref[...] = out


def _attn_paged(q, k_new, v_new, cache_kt, cache_vt, page_table, bias, n_new):
    n_seq, n_pages = page_table.shape
    hd = q.shape[1]
    n_heads = hd // HEAD_DIM
    rows = n_heads * n_new
    page = cache_kt.shape[3]
    pps = PAGES_PER_STEP
    seq_rows = pl.BlockSpec((n_new, hd), lambda n, j, pt: (n, 0))
    new_spec = pl.BlockSpec((1, page, hd), lambda n, j, pt: (n, 0, 0))

    def page_spec(c):
        return pl.BlockSpec((1, n_heads, HEAD_DIM, page),
                            lambda n, j, pt: (pt[n, n_pages - jnp.maximum(j, 1) * pps + c], 0, 0, 0))

    grid_spec = pltpu.PrefetchScalarGridSpec(
        num_scalar_prefetch=1,
        grid=(n_seq, n_pages // pps + 1),
        in_specs=([seq_rows, new_spec, new_spec] + [page_spec(c) for c in range(pps)] * 2
                  + [pl.BlockSpec((rows, 1), lambda n, j, pt: (0, 0))]),
        out_specs=seq_rows,
        scratch_shapes=[pltpu.VMEM((rows, hd), BF16), pltpu.VMEM((rows, hd), F32), pltpu.VMEM((rows, 1), F32)],
    )
    return pl.pallas_call(
        functools.partial(_attn_paged_kernel, pages_per_step=pps),
        grid_spec=grid_spec,
        out_shape=jax.ShapeDtypeStruct((n_seq * n_new, hd), F32),
        compiler_params=_params(("parallel", "arbitrary")),
        name="attn_paged",
    )(page_table, q, k_new, v_new, *([cache_kt] * pps), *([cache_vt] * pps),
      jnp.repeat(bias, n_new).reshape(rows, 1))


def _s5_layer(xp, xs, n_prompt, h0_re, h0_im, prm):
    (g_pre, g_post, w_in, log_dt, a_re, a_im, b_re, b_im, c_re, c_im, d_skip, w_glu, b_glu, w_out) = prm
    d = xp.shape[1]
    n_groups, n_state = a_re.shape
    gc = b_re.shape[-1]
    width = n_groups * gc
    w1, w2, coef = _ssm_prep(log_dt, a_re, a_im, b_re, b_im, c_re, c_im)
    w_in_b, w_glu_b, w_out_b = w_in.astype(BF16), w_glu.astype(BF16), w_out.astype(BF16)
    uz_outs = [(0, width, 1.0, F32), (width, 2 * width, 1.0, F32)]

    u, z = _norm_matmul(xp, g_pre, w_in_b, uz_outs, 512)
    y, hfin = _ssm_seq(u, w1, w2, coef, n_prompt)
    xp = _s5_post(y, u, z, xp, d_skip, w_glu_b, b_glu, w_out_b, g_post, 512)

    n_seq = h0_re.shape[0]
    t_new = xs.shape[0] // n_seq
    us, zs = _norm_matmul(xs, g_pre, w_in_b, uz_outs, 512)
    u8 = (us.reshape(n_seq, t_new, n_groups, gc).transpose(2, 0, 1, 3)
          .reshape(n_groups, n_seq, t_new * gc).astype(BF16))
    h0 = jnp.concatenate([h0_re, h0_im], axis=-1).transpose(1, 0, 2)
    h0s = jnp.concatenate([h0_im, h0_re], axis=-1).transpose(1, 0, 2)
    y8, hf8 = _ssm_step(u8, h0, h0s, w1, w2, coef)
    ys = y8.reshape(n_groups, n_seq, t_new, gc).transpose(1, 2, 0, 3).reshape(xs.shape[0], width)
    xs = _s5_post(ys, us, zs, xs, d_skip, w_glu_b, b_glu, w_out_b, g_post, 512)
    hf8 = hf8.transpose(1, 0, 2)
    return (xp, xs, hfin[..., :n_state], hfin[..., n_state:], hf8[..., :n_state], hf8[..., n_state:])


def kernel(x_prompt, x_sample, state_ssm_re, state_ssm_im, cache_k, cache_v, page_table, a_norm_pre, a_norm_post, a_w_in, a_log_dt, a_A_re, a_A_im, a_B_re, a_B_im, a_C_re, a_C_im, a_D, a_w_glu, a_b_glu, a_w_out, kv_norm, w_kv, b_norm_pre, b_norm_post, b_w_in, b_logit_bias, b_w_out):
    n_prompt, seq, d = x_prompt.shape
    n_seq, t_new, _ = x_sample.shape
    n_a = a_w_in.shape[0]
    n_b = b_w_in.shape[0]
    att = w_kv.shape[1] // 2
    n_heads = att // HEAD_DIM
    assert t_new * 2 == SSM_CHUNK and cache_k.shape[1] == PAGE_SIZE and HEAD_DIM ** -0.5 == 0.125

    xp = x_prompt.reshape(n_prompt * seq, d)
    xs = x_sample.reshape(n_seq * t_new, d)
    p_re, p_im, s_re, s_im = [], [], [], []
    for i in range(n_a):
        prm = (a_norm_pre[i], a_norm_post[i], a_w_in[i], a_log_dt[i], a_A_re[i], a_A_im[i], a_B_re[i], a_B_im[i],
               a_C_re[i], a_C_im[i], a_D[i], a_w_glu[i], a_b_glu[i], a_w_out[i])
        xp, xs, hr, hi, sr, si = _s5_layer(xp, xs, n_prompt, state_ssm_re[i], state_ssm_im[i], prm)
        p_re.append(hr)
        p_im.append(hi)
        s_re.append(sr)
        s_im.append(si)

    w_kv_b = w_kv.astype(BF16)
    kt_p, vt_p, kt_pb, vt_pb = _kv_proj(xp, kv_norm, w_kv_b.T, n_prompt, 512)
    k_p = kt_p.reshape(n_prompt, n_heads, HEAD_DIM, seq).transpose(0, 3, 1, 2)
    v_p = vt_p.reshape(n_prompt, n_heads, HEAD_DIM, seq).transpose(0, 3, 1, 2)
    k_s, v_s = _norm_matmul(xs, kv_norm, w_kv_b, [(0, att, 1.0, F32), (att, 2 * att, 1.0, F32)], 512)
    pad_new = lambda a: jnp.pad(a.reshape(n_seq, t_new, att), ((0, 0), (0, PAGE_SIZE - t_new), (0, 0)))
    k_s_pad, v_s_pad = pad_new(k_s), pad_new(v_s)
    cache_kt = cache_k.transpose(0, 2, 3, 1)
    cache_vt = cache_v.transpose(0, 2, 3, 1)

    scale = HEAD_DIM ** -0.5
    for j in range(n_b):
        w_in_b, w_out_b = b_w_in[j].astype(BF16), b_w_out[j].astype(BF16)
        q, g = _norm_matmul(xp, b_norm_pre[j], w_in_b, [(0, att, scale, BF16), (att, 2 * att, 1.0, F32)], 512)
        o = _attn_prompt(q.reshape(n_prompt, seq, att), kt_pb, vt_pb, b_logit_bias[j])
        xp = _sb_post(o.reshape(n_prompt * seq, att), g, xp, w_out_b, b_norm_post[j], 512)
        qs, gs = _norm_matmul(xs, b_norm_pre[j], w_in_b, [(0, att, scale, F32), (att, 2 * att, 1.0, F32)], 512)
        os_ = _attn_paged(qs, k_s_pad, v_s_pad, cache_kt, cache_vt, page_table, b_logit_bias[j], t_new)
        xs = _sb_post(os_, gs, xs, w_out_b, b_norm_post[j], 512)

    return (xp.reshape(n_prompt, seq, d), xs.reshape(n_seq, t_new, d),
            jnp.stack(p_re), jnp.stack(p_im),
            k_p, v_p,
            jnp.stack(s_re), jnp.stack(s_im),
            k_s.reshape(n_seq, t_new, n_heads, HEAD_DIM), v_s.reshape(n_seq, t_new, n_heads, HEAD_DIM))
```

```python
import functools

import jax
import jax.numpy as jnp
from jax import lax
from jax.experimental import pallas as pl
from jax.experimental.pallas import tpu as pltpu

F32 = jnp.float32
BF16 = jnp.bfloat16

RMS_EPS = 1e-6
SSM_GROUP = 16
SSM_STATE = 64
SSM_CHUNK = 16
SSM_GROUP_BLOCK = 8
LAM_POWERS = 5
TAB_ROWS = 16
HEAD_DIM = 64
LANES = 128
HEADS_PER_LANE_TILE = LANES // HEAD_DIM
ATT_BLOCK = 256
Q_SUBS = 4
BIAS_TERMS = 3
PAGE_SIZE = 128
PAGES_PER_STEP = 8
VMEM_LIMIT = 48 * 1024 * 1024


def _params(semantics):
    return pltpu.CompilerParams(dimension_semantics=semantics, vmem_limit_bytes=VMEM_LIMIT)


def _rms_scale(x):
    return lax.rsqrt(jnp.mean(x * x, axis=-1, keepdims=True) + RMS_EPS)


def _norm_matmul_kernel(x_ref, g_ref, w_ref, *o_refs, splits):
    x = x_ref[...]
    h = (x * _rms_scale(x)) * g_ref[...]
    acc = jnp.dot(h.astype(BF16), w_ref[...], preferred_element_type=F32)
    for o_ref, (lo, hi, scale) in zip(o_refs, splits):
        v = acc[:, lo:hi]
        if scale != 1.0:
            v = v * scale
        o_ref[...] = v.astype(o_ref.dtype)


def _norm_matmul(x, g, w, outs, block_rows):
    m, d = x.shape
    n = w.shape[1]
    tm = min(block_rows, m)
    splits = tuple((lo, hi, scale) for lo, hi, scale, _ in outs)
    return pl.pallas_call(
        functools.partial(_norm_matmul_kernel, splits=splits),
        grid=(m // tm,),
        in_specs=[pl.BlockSpec((tm, d), lambda i: (i, 0)),
                  pl.BlockSpec((1, d), lambda i: (0, 0)),
                  pl.BlockSpec((d, n), lambda i: (0, 0))],
        out_specs=[pl.BlockSpec((tm, hi - lo), lambda i: (i, 0)) for lo, hi, _, _ in outs],
        out_shape=[jax.ShapeDtypeStruct((m, hi - lo), dt) for lo, hi, _, dt in outs],
        compiler_params=_params(("parallel",)),
        name="norm_matmul",
    )(x, g.reshape(1, d), w)


def _kv_proj_kernel(x_ref, g_ref, wt_ref, kt_ref, vt_ref, ktb_ref, vtb_ref):
    x = x_ref[...]
    h = (x * _rms_scale(x)) * g_ref[...]
    acc = lax.dot_general(wt_ref[...], h.astype(BF16), (((1,), (1,)), ((), ())), preferred_element_type=F32)
    att = kt_ref.shape[1]
    n_blocks = ktb_ref.shape[2]
    for t_ref, b_ref, lo in ((kt_ref, ktb_ref, 0), (vt_ref, vtb_ref, att)):
        part = acc[lo:lo + att]
        t_ref[0] = part
        for c in range(n_blocks):
            blk = part[:, c * ATT_BLOCK:(c + 1) * ATT_BLOCK]
            b_ref[0, :, c] = blk.reshape(att // LANES, LANES, ATT_BLOCK).astype(b_ref.dtype)


def _kv_proj(x, g, wt, n_prompt, block_rows):
    m, d = x.shape
    att = wt.shape[0] // 2
    seq = m // n_prompt
    tm = min(block_rows, seq)
    steps = seq // tm
    n_blocks = tm // ATT_BLOCK
    t_spec = pl.BlockSpec((1, att, tm), lambda b, i: (b, 0, i))
    b_spec = pl.BlockSpec((1, att // LANES, n_blocks, LANES, ATT_BLOCK), lambda b, i: (b, 0, i, 0, 0))
    blocked = jax.ShapeDtypeStruct((n_prompt, att // LANES, seq // ATT_BLOCK, LANES, ATT_BLOCK), BF16)
    return pl.pallas_call(
        _kv_proj_kernel,
        grid=(n_prompt, steps),
        in_specs=[pl.BlockSpec((tm, d), lambda b, i: (b * steps + i, 0)),
                  pl.BlockSpec((1, d), lambda b, i: (0, 0)),
                  pl.BlockSpec((2 * att, d), lambda b, i: (0, 0))],
        out_specs=[t_spec, t_spec, b_spec, b_spec],
        out_shape=[jax.ShapeDtypeStruct((n_prompt, att, seq), F32)] * 2 + [blocked] * 2,
        compiler_params=_params(("parallel", "parallel")),
        name="kv_proj",
    )(x, g.reshape(1, d), wt)


def _ssm_lam_kernel(log_dt_ref, a_re_ref, a_im_ref, tab_ref):
    a_re = a_re_ref[...]
    a_im = a_im_ref[...]
    dt = jnp.exp(log_dt_ref[...])
    mag = jnp.exp(dt * a_re)
    lr = mag * jnp.cos(dt * a_im)
    li = mag * jnp.sin(dt * a_im)
    den = a_re * a_re + a_im * a_im
    tab_ref[0] = ((lr - 1.0) * a_re + li * a_im) / den
    tab_ref[1] = (li * a_re - (lr - 1.0) * a_im) / den
    for b in range(LAM_POWERS):
        tab_ref[2 + 2 * b] = lr
        tab_ref[3 + 2 * b] = li
        lr, li = lr * lr - li * li, 2.0 * (lr * li)


def _lam_pow(tau, pow_re, pow_im, shape):
    pr = pi = None
    for b in range(LAM_POWERS - 1):
        on = ((tau >> b) & 1) == 1
        fr = jnp.where(on, jnp.broadcast_to(pow_re[b], shape), 1.0)
        fi = jnp.where(on, jnp.broadcast_to(pow_im[b], shape), 0.0)
        pr, pi = (fr, fi) if pr is None else (pr * fr - pi * fi, pr * fi + pi * fr)
    return pr, pi


def _ssm_prep_kernel(colp_ref, rowp_ref, ct_re_ref, ct_im_ref, bt_re_ref, bt_im_ref,
                     w1_ref, w2_ref, coef_ref):
    n_state2 = 2 * SSM_STATE
    n_in = SSM_CHUNK * SSM_GROUP
    col = lambda i: colp_ref[0, :, i:i + 1]
    tau_l = lax.broadcasted_iota(jnp.int32, (1, n_in), 1) // SSM_GROUP
    row_is_re = lax.broadcasted_iota(jnp.int32, (n_state2, 1), 0) < SSM_STATE
    rep_lanes = (lax.broadcasted_iota(jnp.int32, (SSM_GROUP, n_in), 1) % SSM_GROUP
                 == lax.broadcasted_iota(jnp.int32, (SSM_GROUP, n_in), 0)).astype(F32)
    spread = lambda a: jnp.dot(jnp.concatenate([a, a], axis=0), rep_lanes, preferred_element_type=F32,
                               precision=lax.Precision.HIGHEST)
    ct_re = spread(ct_re_ref[0])
    ct_im = spread(ct_im_ref[0])
    pr, pi = _lam_pow(tau_l, [col(2 + 2 * b) for b in range(LAM_POWERS)],
                      [col(3 + 2 * b) for b in range(LAM_POWERS)], (n_state2, n_in))
    g_re, g_im = ct_re * pr - ct_im * pi, ct_re * pi + ct_im * pr
    rhs_k = jnp.where(row_is_re, g_re, g_im)
    q_re, q_im = g_re * col(2) - g_im * col(3), g_re * col(3) + g_im * col(2)
    w2_ref[0] = jnp.where(row_is_re, q_re, -q_im).astype(w2_ref.dtype)

    row = lambda i: rowp_ref[0, i:i + 1, :]
    lane_is_re = lax.broadcasted_iota(jnp.int32, (1, n_state2), 1) < SSM_STATE
    w_re, w_im = row(0), row(1)
    rep_halves = (lax.broadcasted_iota(jnp.int32, (SSM_STATE, n_state2), 1) % SSM_STATE
                  == lax.broadcasted_iota(jnp.int32, (SSM_STATE, n_state2), 0)).astype(F32)
    stack = lambda a: jnp.concatenate([jnp.dot(a, rep_halves, preferred_element_type=F32,
                                               precision=lax.Precision.HIGHEST)] * SSM_CHUNK, axis=0)
    bt_re = stack(bt_re_ref[0])
    bt_im = stack(bt_im_ref[0])
    bb_re = w_re * bt_re - w_im * bt_im
    bb_im = w_re * bt_im + w_im * bt_re

    lhs_k = jnp.where(lane_is_re, bb_re[0:SSM_GROUP], -bb_im[0:SSM_GROUP])
    kk = jnp.dot(lhs_k, rhs_k, preferred_element_type=F32, precision=lax.Precision.HIGHEST)
    lane_in = lax.broadcasted_iota(jnp.int32, (1, n_in), 1)
    for s in range(SSM_CHUNK):
        shifted = kk if s == 0 else pltpu.roll(kk, s * SSM_GROUP, 1)
        piece = jnp.where(lane_in >= s * SSM_GROUP, shifted, 0.0)
        w1_ref[0, s * SSM_GROUP:(s + 1) * SSM_GROUP, 0:n_in] = piece.astype(w1_ref.dtype)

    tau_row = SSM_CHUNK - 1 - lax.broadcasted_iota(jnp.int32, (n_in, 1), 0) // SSM_GROUP
    pr_r, pi_r = _lam_pow(tau_row, [row(2 + 2 * b) for b in range(LAM_POWERS)],
                          [row(3 + 2 * b) for b in range(LAM_POWERS)], (n_in, n_state2))
    p_re = bb_re * pr_r - bb_im * pi_r
    p_im = bb_re * pi_r + bb_im * pr_r
    w1_ref[0, :, n_in:n_in + n_state2] = jnp.where(lane_is_re, p_re, p_im).astype(w1_ref.dtype)
    w1_ref[0, :, n_in + n_state2:n_in + 2 * n_state2] = jnp.where(lane_is_re, p_im, p_re).astype(w1_ref.dtype)

    full, half = 2 * (LAM_POWERS - 1), 2 * (LAM_POWERS - 2)
    zero = jnp.zeros_like(w_re)
    coef_ref[0] = jnp.concatenate(
        [row(2 + full), jnp.where(lane_is_re, -row(3 + full), row(3 + full)),
         row(2 + half), jnp.where(lane_is_re, -row(3 + half), row(3 + half)), zero, zero, zero, zero], axis=0)


def _ssm_prep(log_dt, a_re, a_im, b_re, b_im, c_re, c_im):
    g, p = a_re.shape
    c = b_re.shape[-1]
    n_in = SSM_CHUNK * c
    n_tab = 2 + 2 * LAM_POWERS
    tab = pl.pallas_call(
        _ssm_lam_kernel,
        out_shape=jax.ShapeDtypeStruct((n_tab, g, p), F32),
        name="ssm_lam",
    )(log_dt.reshape(g, 1), a_re, a_im)
    tab2 = jnp.concatenate([tab, tab], axis=-1)
    pad = TAB_ROWS - n_tab
    cols = jnp.pad(tab2.transpose(1, 2, 0), ((0, 0), (0, 0), (0, pad)))
    rows = jnp.pad(tab2.transpose(1, 0, 2), ((0, 0), (0, pad), (0, 0)))
    ct = lambda a: jnp.swapaxes(a, 1, 2)
    bt = lambda a: jnp.swapaxes(a, 1, 2)
    spec3 = lambda s1, s2: pl.BlockSpec((1, s1, s2), lambda i: (i, 0, 0))
    w1, w2, coef = pl.pallas_call(
        _ssm_prep_kernel,
        grid=(g,),
        in_specs=[spec3(2 * p, TAB_ROWS), spec3(TAB_ROWS, 2 * p), spec3(p, c), spec3(p, c), spec3(c, p), spec3(c, p)],
        out_specs=[spec3(n_in, n_in + 4 * p), spec3(2 * p, n_in), spec3(8, 2 * p)],
        out_shape=[jax.ShapeDtypeStruct((g, n_in, n_in + 4 * p), BF16),
                   jax.ShapeDtypeStruct((g, 2 * p, n_in), BF16),
                   jax.ShapeDtypeStruct((g, 8, 2 * p), F32)],
        compiler_params=_params(("parallel",)),
        name="ssm_prep",
    )(cols, rows, ct(c_re), ct(c_im), bt(b_re), bt(b_im))
    return w1, w2, jnp.swapaxes(coef, 0, 1)[:4]


def _swap_array_and_lane_block(xs, lane_block_width):
    n = len(xs)
    lanes = n * lane_block_width
    blk = lax.broadcasted_iota(jnp.int32, (1, lanes), 1) // lane_block_width
    d = n // 2
    while d >= 1:
        keep = (blk & d) == 0
        new = list(xs)
        for i in range(n):
            if i & d == 0:
                lo, hi = xs[i], xs[i + d]
                new[i] = jnp.where(keep, lo, pltpu.roll(hi, d * lane_block_width, 1))
                new[i + d] = jnp.where(keep, pltpu.roll(lo, lanes - d * lane_block_width, 1), hi)
        xs = new
        d //= 2
    return xs


def _ssm_seq_kernel(u_ref, w1_ref, w2_ref, coef_ref, y_ref, hfin_ref, s_ref, ss_ref, yc_ref):
    gb = SSM_GROUP_BLOCK
    nk = u_ref.shape[0] // SSM_CHUNK
    n_in = SSM_CHUNK * SSM_GROUP
    n_state2 = 2 * SSM_STATE
    steps_per_tile = LANES // SSM_GROUP
    n_tiles = SSM_CHUNK // steps_per_tile
    assert gb * SSM_GROUP == LANES and steps_per_tile == gb

    u_cm = []
    for t in range(n_tiles):
        by_step = [u_ref[pl.ds(t * steps_per_tile + i, nk, stride=SSM_CHUNK), :] for i in range(steps_per_tile)]
        u_cm.append(_swap_array_and_lane_block(by_step, SSM_GROUP))
    for g in range(gb):
        u_g = jnp.concatenate([u_cm[t][g] for t in range(n_tiles)], axis=1).astype(BF16)
        r = jnp.dot(u_g, w1_ref[g], preferred_element_type=F32)
        yc_ref[g] = r[:, 0:n_in]
        s_ref[pl.ds(g, nk, stride=gb), :] = r[:, n_in:n_in + n_state2]
        ss_ref[pl.ds(g, nk, stride=gb), :] = r[:, n_in + n_state2:n_in + 2 * n_state2]
    a = coef_ref[0]
    b = coef_ref[1]

    def step(k, carry):
        x, xs = carry
        r0 = pl.multiple_of(k * gb, gb)
        add = s_ref[pl.ds(r0, gb), :]
        adds = ss_ref[pl.ds(r0, gb), :]
        s_ref[pl.ds(r0, gb), :] = x
        return a * x + b * xs + add, a * xs - b * x + adds

    zero = jnp.zeros((gb, n_state2), F32)
    x_fin, _ = lax.fori_loop(0, nk, step, (zero, zero))
    hfin_ref[0] = x_fin
    for g in range(gb):
        h_prev = s_ref[pl.ds(g, nk, stride=gb), :]
        yc_ref[g] += jnp.dot(h_prev.astype(BF16), w2_ref[g], preferred_element_type=F32)
    for t in range(n_tiles):
        by_group = [yc_ref[g, :, t * LANES:(t + 1) * LANES] for g in range(gb)]
        by_step = _swap_array_and_lane_block(by_group, SSM_GROUP)
        for i in range(steps_per_tile):
            y_ref[pl.ds(t * steps_per_tile + i, nk, stride=SSM_CHUNK), :] = by_step[i]


def _ssm_seq(u, w1, w2, coef, n_seq):
    m, width = u.shape
    seq = m // n_seq
    g = w1.shape[0]
    gb = SSM_GROUP_BLOCK
    nk = seq // SSM_CHUNK
    n_in = SSM_CHUNK * SSM_GROUP
    n_state2 = 2 * SSM_STATE
    tok = pl.BlockSpec((seq, LANES), lambda i, j: (i, j))
    return pl.pallas_call(
        _ssm_seq_kernel,
        grid=(n_seq, g // gb),
        in_specs=[tok,
                  pl.BlockSpec((gb, n_in, n_in + 2 * n_state2), lambda i, j: (j, 0, 0)),
                  pl.BlockSpec((gb, n_state2, n_in), lambda i, j: (j, 0, 0)),
                  pl.BlockSpec((4, gb, n_state2), lambda i, j: (0, j, 0))],
        out_specs=[tok, pl.BlockSpec((1, gb, n_state2), lambda i, j: (i, j, 0))],
        out_shape=[jax.ShapeDtypeStruct((m, width), F32),
                   jax.ShapeDtypeStruct((n_seq, g, n_state2), F32)],
        scratch_shapes=[pltpu.VMEM((nk * gb, n_state2), F32), pltpu.VMEM((nk * gb, n_state2), F32),
                        pltpu.VMEM((gb, nk, n_in), F32)],
        compiler_params=_params(("parallel", "parallel")),
        name="ssm_seq",
    )(u, w1, w2, coef)


def _ssm_step_kernel(u_ref, h0_ref, h0s_ref, w1_ref, w2_ref, coef_ref, y_ref, hfin_ref):
    gb = u_ref.shape[0]
    n_half = u_ref.shape[2]
    n_in = 2 * n_half
    n_state2 = 2 * SSM_STATE
    for g in range(gb):
        u = u_ref[g]
        h0 = h0_ref[g]
        t_half = w1_ref[g, 0:n_half, 0:n_half]
        p_half = w1_ref[g, n_half:n_in, n_in:n_in + n_state2]
        q_half = w2_ref[g, :, 0:n_half]
        y_ref[g] = (jnp.dot(u, t_half, preferred_element_type=F32)
                    + jnp.dot(h0.astype(BF16), q_half, preferred_element_type=F32))
        hfin_ref[g] = (coef_ref[2, g:g + 1, :] * h0 + coef_ref[3, g:g + 1, :] * h0s_ref[g]
                       + jnp.dot(u, p_half, preferred_element_type=F32))


def _ssm_step(u8, h0, h0s, w1, w2, coef):
    g, n, n_half = u8.shape
    gb = SSM_GROUP_BLOCK
    n_state2 = 2 * SSM_STATE
    n_in = 2 * n_half
    blk = lambda s1, s2: pl.BlockSpec((gb, s1, s2), lambda j: (j, 0, 0))
    return pl.pallas_call(
        _ssm_step_kernel,
        grid=(g // gb,),
        in_specs=[blk(n, n_half), blk(n, n_state2), blk(n, n_state2),
                  blk(n_in, n_in + 2 * n_state2), blk(n_state2, n_in),
                  pl.BlockSpec((4, gb, n_state2), lambda j: (0, j, 0))],
        out_specs=[blk(n, n_half), blk(n, n_state2)],
        out_shape=[jax.ShapeDtypeStruct((g, n, n_half), F32), jax.ShapeDtypeStruct((g, n, n_state2), F32)],
        compiler_params=_params(("parallel",)),
        name="ssm_step",
    )(u8, h0, h0s, w1, w2, coef)


def _s5_post_kernel(y_ref, u_ref, z_ref, x_ref, d_ref, wg_ref, bg_ref, wo_ref, gp_ref, o_ref):
    y = jax.nn.gelu(y_ref[...] + d_ref[...] * u_ref[...])
    gate = jnp.dot(y.astype(BF16), wg_ref[...], preferred_element_type=F32) + bg_ref[...]
    y = y * jax.nn.sigmoid(gate)
    z = z_ref[...]
    y = y * (z * jax.nn.sigmoid(z))
    o = jnp.dot(y.astype(BF16), wo_ref[...], preferred_element_type=F32)
    o_ref[...] = x_ref[...] + (o * _rms_scale(o)) * gp_ref[...]


def _s5_post(y, u, z, x, d_skip, w_glu, b_glu, w_out, g_post, block_rows):
    m, d = x.shape
    tm = min(block_rows, m)
    row = pl.BlockSpec((tm, d), lambda i: (i, 0))
    vec = pl.BlockSpec((1, d), lambda i: (0, 0))
    mat = pl.BlockSpec((d, d), lambda i: (0, 0))
    return pl.pallas_call(
        _s5_post_kernel,
        grid=(m // tm,),
        in_specs=[row, row, row, row, vec, mat, vec, mat, vec],
        out_specs=row,
        out_shape=jax.ShapeDtypeStruct((m, d), F32),
        compiler_params=_params(("parallel",)),
        name="s5_post",
    )(y, u, z, x, d_skip.reshape(1, d), w_glu, b_glu.reshape(1, d), w_out, g_post.reshape(1, d))


def _sb_post_kernel(o_ref, g_ref, x_ref, wo_ref, gp_ref, out_ref):
    g = g_ref[...]
    y = o_ref[...] * (g * jax.nn.sigmoid(g))
    r = jnp.dot(y.astype(BF16), wo_ref[...], preferred_element_type=F32)
    out_ref[...] = x_ref[...] + (r * _rms_scale(r)) * gp_ref[...]


def _sb_post(o, g, x, w_out, g_post, block_rows):
    m, d = x.shape
    tm = min(block_rows, m)
    row = pl.BlockSpec((tm, d), lambda i: (i, 0))
    return pl.pallas_call(
        _sb_post_kernel,
        grid=(m // tm,),
        in_specs=[row, row, row, pl.BlockSpec((d, d), lambda i: (0, 0)), pl.BlockSpec((1, d), lambda i: (0, 0))],
        out_specs=row,
        out_shape=jax.ShapeDtypeStruct((m, d), F32),
        compiler_params=_params(("parallel",)),
        name="sb_post",
    )(o, g, x, w_out, g_post.reshape(1, d))


def _suffix_mask(n):
    j = lax.broadcasted_iota(jnp.int32, (2 * n, n), 0)
    s = lax.broadcasted_iota(jnp.int32, (2 * n, n), 1)
    return ((j >= s) & ((j < n) | (j >= s + n))).astype(BF16)


def _miss(z, mask):
    sp = jnp.maximum(z, 0.0) + jnp.log(1.0 + jnp.exp(-jnp.abs(z)))
    return sp if mask is None else jnp.where(mask, sp, 0.0)


def _suffix_sums(miss, tri):
    hi = miss.astype(BF16)
    lo = (miss - hi.astype(F32)).astype(BF16)
    return jnp.dot(jnp.concatenate([hi, lo], axis=1), tri, preferred_element_type=F32)


def _stick_weights(z, misses_from_here, mask):
    w = jnp.exp(z - misses_from_here)
    return w if mask is None else jnp.where(mask, w, 0.0)


def _attn_prompt_kernel(bias_ref, q_ref, kt_ref, vt_ref, o_ref, acc_ref, later_ref):
    tb = ATT_BLOCK
    hp = pl.program_id(1)
    qi = pl.program_id(2)
    q = q_ref[0]
    lane = lax.broadcasted_iota(jnp.int32, (1, LANES), 1)
    tri = _suffix_mask(tb)
    diag = lax.broadcasted_iota(jnp.int32, (tb, tb), 1) < lax.broadcasted_iota(jnp.int32, (tb, tb), 0)
    heads = range(HEADS_PER_LANE_TILE)
    q_aug = []
    for hh in heads:
        qm = jnp.where((lane >= hh * HEAD_DIM) & (lane < (hh + 1) * HEAD_DIM), q, jnp.zeros_like(q))
        rest = jnp.full((1, LANES), bias_ref[hp * HEADS_PER_LANE_TILE + hh], F32)
        terms = jnp.zeros((1, LANES), F32)
        for t in range(BIAS_TERMS):
            term = rest.astype(BF16).astype(F32)
            terms = jnp.where(lane == t, term, terms)
            rest = rest - term
        q_aug.append(jnp.concatenate([qm, jnp.broadcast_to(terms, qm.shape).astype(BF16)], axis=1))
    ones_rows = (lax.broadcasted_iota(jnp.int32, (LANES, tb), 0) < BIAS_TERMS).astype(BF16)
    neg_tri = -tri
    acc_ref[...] = jnp.zeros_like(acc_ref)
    later_ref[...] = jnp.zeros_like(later_ref)

    def visit(kj, subs_masks):
        kb = jnp.concatenate([kt_ref[0, 0, kj], ones_rows], axis=0)
        kb_minus_tri = jnp.concatenate([kb, neg_tri], axis=0)
        vb = vt_ref[0, 0, kj]
        chains = [(hh, slice(sub * tb, (sub + 1) * tb), mask) for hh in heads for sub, mask in subs_masks]
        zs = [jnp.dot(q_aug[hh][rows], kb, preferred_element_type=F32) for hh, rows, _ in chains]
        hi_lo, z0 = [], []
        for z, (_, _, mask) in zip(zs, chains):
            miss = _miss(z, mask)
            hi = miss.astype(BF16)
            hi_lo.append(jnp.concatenate([hi, (miss - hi.astype(F32)).astype(BF16)], axis=1))
            z0.append(z[:, 0:1])
        log_ws = [jnp.dot(jnp.concatenate([q_aug[hh][rows], hl], axis=1), kb_minus_tri, preferred_element_type=F32)
                  for hl, (hh, rows, _) in zip(hi_lo, chains)]
        for lw, z_first, (hh, rows, mask) in zip(log_ws, z0, chains):
            w = jnp.exp(lw)
            w = (w if mask is None else jnp.where(mask, w, 0.0)).astype(BF16)
            c = later_ref[hh, rows]
            pv = lax.dot_general(w, vb, (((1,), (1,)), ((), ())), preferred_element_type=F32)
            acc_ref[hh, rows] += jnp.exp(-c) * pv
            later_ref[hh, rows] = c + (z_first - lw[:, 0:1])

    first = qi * Q_SUBS
    for lead in reversed(range(Q_SUBS)):
        visit(first + lead, [(lead, diag)] + [(sub, None) for sub in range(lead + 1, Q_SUBS)])

    def body(t, carry):
        visit(first - 1 - t, [(sub, None) for sub in range(Q_SUBS)])
        return carry

    lax.fori_loop(0, first, body, 0)
    o_ref[0] = jnp.where(lane < HEAD_DIM, acc_ref[0], acc_ref[1])


def _attn_prompt(q, kt, vt, bias):
    b, s, hd = q.shape
    n_kb = kt.shape[2]
    tq = Q_SUBS * ATT_BLOCK
    qspec = pl.BlockSpec((1, tq, LANES), lambda i, h, j: (i, j, h))
    kspec = pl.BlockSpec((1, 1, n_kb, LANES, ATT_BLOCK), lambda i, h, j: (i, h, 0, 0, 0))
    return pl.pallas_call(
        _attn_prompt_kernel,
        grid=(b, hd // LANES, s // tq),
        in_specs=[pl.BlockSpec(memory_space=pltpu.SMEM), qspec, kspec, kspec],
        out_specs=qspec,
        out_shape=jax.ShapeDtypeStruct((b, s, hd), F32),
        scratch_shapes=[pltpu.VMEM((HEADS_PER_LANE_TILE, tq, LANES), F32),
                        pltpu.VMEM((HEADS_PER_LANE_TILE, tq, 1), F32)],
        compiler_params=_params(("parallel", "parallel", "arbitrary")),
        name="attn_prompt",
    )(bias, q, kt, vt)


def _attn_paged_kernel(pt_ref, q_ref, knew_ref, vnew_ref, *rest, pages_per_step):
    del pt_ref
    kpage_refs = rest[:pages_per_step]
    vpage_refs = rest[pages_per_step:2 * pages_per_step]
    bias_ref, o_ref, qbd_ref, acc_ref, later_ref = rest[2 * pages_per_step:]
    j = pl.program_id(1)
    nq = q_ref.shape[0]
    hd = q_ref.shape[1]
    n_heads = hd // HEAD_DIM
    rows = n_heads * nq
    tk = knew_ref.shape[1]
    row = lax.broadcasted_iota(jnp.int32, (rows, 1), 0)
    tri = _suffix_mask(tk)
    contract_last = (((1,), (1,)), ((), ()))

    @pl.when(j == 0)
    def _():
        lane = lax.broadcasted_iota(jnp.int32, (1, hd), 1)
        q_rep = jnp.concatenate([q_ref[...]] * n_heads, axis=0)
        qbd = jnp.where(lane // HEAD_DIM == row // nq, q_rep, 0.0).astype(BF16)
        qbd_ref[...] = qbd
        col = lax.broadcasted_iota(jnp.int32, (1, tk), 1)
        mask = col < row % nq
        z = lax.dot_general(qbd, knew_ref[0].astype(BF16), contract_last, preferred_element_type=F32) + bias_ref[...]
        sums = _suffix_sums(_miss(z, mask), tri)
        w = _stick_weights(z, sums, mask).astype(BF16)
        acc_ref[...] = jnp.dot(w, vnew_ref[0].astype(BF16), preferred_element_type=F32)
        later_ref[...] = sums[:, 0:1]

    @pl.when(j > 0)
    def _():
        order = list(reversed(range(pages_per_step)))
        zs = [jnp.dot(qbd_ref[...], kpage_refs[c][0].reshape(hd, tk).astype(BF16), preferred_element_type=F32)
              + bias_ref[...] for c in order]
        sums = [_suffix_sums(_miss(z, None), tri) for z in zs]
        later_blocks = later_ref[...]
        ws = []
        for z, sm in zip(zs, sums):
            ws.append(_stick_weights(z, sm + later_blocks, None).astype(BF16))
            later_blocks = later_blocks + sm[:, 0:1]
        later_ref[...] = later_blocks
        v_all = jnp.concatenate([vpage_refs[c][0].reshape(hd, tk).astype(BF16) for c in order], axis=1)
        acc_ref[...] += lax.dot_general(jnp.concatenate(ws, axis=1), v_all, contract_last,
                                        preferred_element_type=F32)

    @pl.when(j == pl.num_programs(1) - 1)
    def _():
        lane = lax.broadcasted_iota(jnp.int32, (1, hd), 1)
        out = jnp.zeros((nq, hd), F32)
        for h in range(n_heads):
            out = out + jnp.where(lane // HEAD_DIM == h, acc_ref[h * nq:(h + 1) * nq, :], 0.0)
        o_ref[...] = out


def _attn_paged(q, k_new, v_new, cache_kt, cache_vt, page_table, bias, n_new):
    n_seq, n_pages = page_table.shape
    hd = q.shape[1]
    n_heads = hd // HEAD_DIM
    rows = n_heads * n_new
    page = cache_kt.shape[3]
    pps = PAGES_PER_STEP
    seq_rows = pl.BlockSpec((n_new, hd), lambda n, j, pt: (n, 0))
    new_spec = pl.BlockSpec((1, page, hd), lambda n, j, pt: (n, 0, 0))

    def page_spec(c):
        return pl.BlockSpec((1, n_heads, HEAD_DIM, page),
                            lambda n, j, pt: (pt[n, n_pages - jnp.maximum(j, 1) * pps + c], 0, 0, 0))

    grid_spec = pltpu.PrefetchScalarGridSpec(
        num_scalar_prefetch=1,
        grid=(n_seq, n_pages // pps + 1),
        in_specs=([seq_rows, new_spec, new_spec] + [page_spec(c) for c in range(pps)] * 2
                  + [pl.BlockSpec((rows, 1), lambda n, j, pt: (0, 0))]),
        out_specs=seq_rows,
        scratch_shapes=[pltpu.VMEM((rows, hd), BF16), pltpu.VMEM((rows, hd), F32), pltpu.VMEM((rows, 1), F32)],
    )
    return pl.pallas_call(
        functools.partial(_attn_paged_kernel, pages_per_step=pps),
        grid_spec=grid_spec,
        out_shape=jax.ShapeDtypeStruct((n_seq * n_new, hd), F32),
        compiler_params=_params(("parallel", "arbitrary")),
        name="attn_paged",
    )(page_table, q, k_new, v_new, *([cache_kt] * pps), *([cache_vt] * pps),
      jnp.repeat(bias, n_new).reshape(rows, 1))


def _s5_layer(xp, xs, n_prompt, h0_re, h0_im, prm):
    (g_pre, g_post, w_in, log_dt, a_re, a_im, b_re, b_im, c_re, c_im, d_skip, w_glu, b_glu, w_out) = prm
    d = xp.shape[1]
    n_groups, n_state = a_re.shape
    gc = b_re.shape[-1]
    width = n_groups * gc
    w1, w2, coef = _ssm_prep(log_dt, a_re, a_im, b_re, b_im, c_re, c_im)
    w_in_b, w_glu_b, w_out_b = w_in.astype(BF16), w_glu.astype(BF16), w_out.astype(BF16)
    uz_outs = [(0, width, 1.0, F32), (width, 2 * width, 1.0, F32)]

    u, z = _norm_matmul(xp, g_pre, w_in_b, uz_outs, 512)
    y, hfin = _ssm_seq(u, w1, w2, coef, n_prompt)
    xp = _s5_post(y, u, z, xp, d_skip, w_glu_b, b_glu, w_out_b, g_post, 512)

    n_seq = h0_re.shape[0]
    t_new = xs.shape[0] // n_seq
    us, zs = _norm_matmul(xs, g_pre, w_in_b, uz_outs, 512)
    u8 = (us.reshape(n_seq, t_new, n_groups, gc).transpose(2, 0, 1, 3)
          .reshape(n_groups, n_seq, t_new * gc).astype(BF16))
    h0 = jnp.concatenate([h0_re, h0_im], axis=-1).transpose(1, 0, 2)
    h0s = jnp.concatenate([h0_im, h0_re], axis=-1).transpose(1, 0, 2)
    y8, hf8 = _ssm_step(u8, h0, h0s, w1, w2, coef)
    ys = y8.reshape(n_groups, n_seq, t_new, gc).transpose(1, 2, 0, 3).reshape(xs.shape[0], width)
    xs = _s5_post(ys, us, zs, xs, d_skip, w_glu_b, b_glu, w_out_b, g_post, 512)
    hf8 = hf8.transpose(1, 0, 2)
    return (xp, xs, hfin[..., :n_state], hfin[..., n_state:], hf8[..., :n_state], hf8[..., n_state:])


def kernel(x_prompt, x_sample, state_ssm_re, state_ssm_im, cache_k, cache_v, page_table, a_norm_pre, a_norm_post, a_w_in, a_log_dt, a_A_re, a_A_im, a_B_re, a_B_im, a_C_re, a_C_im, a_D, a_w_glu, a_b_glu, a_w_out, kv_norm, w_kv, b_norm_pre, b_norm_post, b_w_in, b_logit_bias, b_w_out):
    n_prompt, seq, d = x_prompt.shape
    n_seq, t_new, _ = x_sample.shape
    n_a = a_w_in.shape[0]
    n_b = b_w_in.shape[0]
    att = w_kv.shape[1] // 2
    n_heads = att // HEAD_DIM
    assert t_new * 2 == SSM_CHUNK and cache_k.shape[1] == PAGE_SIZE and HEAD_DIM ** -0.5 == 0.125

    xp = x_prompt.reshape(n_prompt * seq, d)
    xs = x_sample.reshape(n_seq * t_new, d)
    p_re, p_im, s_re, s_im = [], [], [], []
    for i in range(n_a):
        prm = (a_norm_pre[i], a_norm_post[i], a_w_in[i], a_log_dt[i], a_A_re[i], a_A_im[i], a_B_re[i], a_B_im[i],
               a_C_re[i], a_C_im[i], a_D[i], a_w_glu[i], a_b_glu[i], a_w_out[i])
        xp, xs, hr, hi, sr, si = _s5_layer(xp, xs, n_prompt, state_ssm_re[i], state_ssm_im[i], prm)
        p_re.append(hr)
        p_im.append(hi)
        s_re.append(sr)
        s_im.append(si)

    w_kv_b = w_kv.astype(BF16)
    kt_p, vt_p, kt_pb, vt_pb = _kv_proj(xp, kv_norm, w_kv_b.T, n_prompt, 512)
    k_p = kt_p.reshape(n_prompt, n_heads, HEAD_DIM, seq).transpose(0, 3, 1, 2)
    v_p = vt_p.reshape(n_prompt, n_heads, HEAD_DIM, seq).transpose(0, 3, 1, 2)
    k_s, v_s = _norm_matmul(xs, kv_norm, w_kv_b, [(0, att, 1.0, F32), (att, 2 * att, 1.0, F32)], 512)
    pad_new = lambda a: jnp.pad(a.reshape(n_seq, t_new, att), ((0, 0), (0, PAGE_SIZE - t_new), (0, 0)))
    k_s_pad, v_s_pad = pad_new(k_s), pad_new(v_s)
    cache_kt = cache_k.transpose(0, 2, 3, 1)
    cache_vt = cache_v.transpose(0, 2, 3, 1)

    scale = HEAD_DIM ** -0.5
    for j in range(n_b):
        w_in_b, w_out_b = b_w_in[j].astype(BF16), b_w_out[j].astype(BF16)
        q, g = _norm_matmul(xp, b_norm_pre[j], w_in_b, [(0, att, scale, BF16), (att, 2 * att, 1.0, F32)], 512)
        o = _attn_prompt(q.reshape(n_prompt, seq, att), kt_pb, vt_pb, b_logit_bias[j])
        xp = _sb_post(o.reshape(n_prompt * seq, att), g, xp, w_out_b, b_norm_post[j], 512)
        qs, gs = _norm_matmul(xs, b_norm_pre[j], w_in_b, [(0, att, scale, F32), (att, 2 * att, 1.0, F32)], 512)
        os_ = _attn_paged(qs, k_s_pad, v_s_pad, cache_kt, cache_vt, page_table, b_logit_bias[j], t_new)
        xs = _sb_post(os_, gs, xs, w_out_b, b_norm_post[j], 512)

    return (xp.reshape(n_prompt, seq, d), xs.reshape(n_seq, t_new, d),
            jnp.stack(p_re), jnp.stack(p_im),
            k_p, v_p,
            jnp.stack(s_re), jnp.stack(s_im),
            k_s.reshape(n_seq, t_new, n_heads, HEAD_DIM), v_s.reshape(n_seq, t_new, n_heads, HEAD_DIM))
```

```python
import functools

import jax
import jax.numpy as jnp
from jax import lax
from jax.experimental import pallas as pl
from jax.experimental.pallas import tpu as pltpu

F32 = jnp.float32
BF16 = jnp.bfloat16

RMS_EPS = 1e-6
SSM_GROUP = 16
SSM_STATE = 64
SSM_CHUNK = 16
SSM_GROUP_BLOCK = 8
LAM_POWERS = 5
TAB_ROWS = 16
HEAD_DIM = 64
LANES = 128
HEADS_PER_LANE_TILE = LANES // HEAD_DIM
ATT_BLOCK = 256
Q_SUBS = 4
BIAS_TERMS = 3
PAGE_SIZE = 128
PAGES_PER_STEP = 8
VMEM_LIMIT = 48 * 1024 * 1024


def _params(semantics):
    return pltpu.CompilerParams(dimension_semantics=semantics, vmem_limit_bytes=VMEM_LIMIT)


def _rms_scale(x):
    return lax.rsqrt(jnp.mean(x * x, axis=-1, keepdims=True) + RMS_EPS)


def _norm_matmul_kernel(x_ref, g_ref, w_ref, *o_refs, splits):
    x = x_ref[...]
    h = (x * _rms_scale(x)) * g_ref[...]
    acc = jnp.dot(h.astype(BF16), w_ref[...], preferred_element_type=F32)
    for o_ref, (lo, hi, scale) in zip(o_refs, splits):
        v = acc[:, lo:hi]
        if scale != 1.0:
            v = v * scale
        o_ref[...] = v.astype(o_ref.dtype)


def _norm_matmul(x, g, w, outs, block_rows):
    m, d = x.shape
    n = w.shape[1]
    tm = min(block_rows, m)
    splits = tuple((lo, hi, scale) for lo, hi, scale, _ in outs)
    return pl.pallas_call(
        functools.partial(_norm_matmul_kernel, splits=splits),
        grid=(m // tm,),
        in_specs=[pl.BlockSpec((tm, d), lambda i: (i, 0)),
                  pl.BlockSpec((1, d), lambda i: (0, 0)),
                  pl.BlockSpec((d, n), lambda i: (0, 0))],
        out_specs=[pl.BlockSpec((tm, hi - lo), lambda i: (i, 0)) for lo, hi, _, _ in outs],
        out_shape=[jax.ShapeDtypeStruct((m, hi - lo), dt) for lo, hi, _, dt in outs],
        compiler_params=_params(("parallel",)),
        name="norm_matmul",
    )(x, g.reshape(1, d), w)


def _kv_proj_kernel(x_ref, g_ref, wt_ref, kt_ref, vt_ref, ktb_ref, vtb_ref):
    x = x_ref[...]
    h = (x * _rms_scale(x)) * g_ref[...]
    acc = lax.dot_general(wt_ref[...], h.astype(BF16), (((1,), (1,)), ((), ())), preferred_element_type=F32)
    att = kt_ref.shape[1]
    n_blocks = ktb_ref.shape[2]
    for t_ref, b_ref, lo in ((kt_ref, ktb_ref, 0), (vt_ref, vtb_ref, att)):
        part = acc[lo:lo + att]
        t_ref[0] = part
        for c in range(n_blocks):
            blk = part[:, c * ATT_BLOCK:(c + 1) * ATT_BLOCK]
            b_ref[0, :, c] = blk.reshape(att // LANES, LANES, ATT_BLOCK).astype(b_ref.dtype)


def _kv_proj(x, g, wt, n_prompt, block_rows):
    m, d = x.shape
    att = wt.shape[0] // 2
    seq = m // n_prompt
    tm = min(block_rows, seq)
    steps = seq // tm
    n_blocks = tm // ATT_BLOCK
    t_spec = pl.BlockSpec((1, att, tm), lambda b, i: (b, 0, i))
    b_spec = pl.BlockSpec((1, att // LANES, n_blocks, LANES, ATT_BLOCK), lambda b, i: (b, 0, i, 0, 0))
    blocked = jax.ShapeDtypeStruct((n_prompt, att // LANES, seq // ATT_BLOCK, LANES, ATT_BLOCK), BF16)
    return pl.pallas_call(
        _kv_proj_kernel,
        grid=(n_prompt, steps),
        in_specs=[pl.BlockSpec((tm, d), lambda b, i: (b * steps + i, 0)),
                  pl.BlockSpec((1, d), lambda b, i: (0, 0)),
                  pl.BlockSpec((2 * att, d), lambda b, i: (0, 0))],
        out_specs=[t_spec, t_spec, b_spec, b_spec],
        out_shape=[jax.ShapeDtypeStruct((n_prompt, att, seq), F32)] * 2 + [blocked] * 2,
        compiler_params=_params(("parallel", "parallel")),
        name="kv_proj",
    )(x, g.reshape(1, d), wt)


def _ssm_lam_kernel(log_dt_ref, a_re_ref, a_im_ref, tab_ref):
    a_re = a_re_ref[...]
    a_im = a_im_ref[...]
    dt = jnp.exp(log_dt_ref[...])
    mag = jnp.exp(dt * a_re)
    lr = mag * jnp.cos(dt * a_im)
    li = mag * jnp.sin(dt * a_im)
    den = a_re * a_re + a_im * a_im
    tab_ref[0] = ((lr - 1.0) * a_re + li * a_im) / den
    tab_ref[1] = (li * a_re - (lr - 1.0) * a_im) / den
    for b in range(LAM_POWERS):
        tab_ref[2 + 2 * b] = lr
        tab_ref[3 + 2 * b] = li
        lr, li = lr * lr - li * li, 2.0 * (lr * li)


def _lam_pow(tau, pow_re, pow_im, shape):
    pr = pi = None
    for b in range(LAM_POWERS - 1):
        on = ((tau >> b) & 1) == 1
        fr = jnp.where(on, jnp.broadcast_to(pow_re[b], shape), 1.0)
        fi = jnp.where(on, jnp.broadcast_to(pow_im[b], shape), 0.0)
        pr, pi = (fr, fi) if pr is None else (pr * fr - pi * fi, pr * fi + pi * fr)
    return pr, pi


def _ssm_prep_kernel(colp_ref, rowp_ref, ct_re_ref, ct_im_ref, bt_re_ref, bt_im_ref,
                     w1_ref, w2_ref, coef_ref):
    n_state2 = 2 * SSM_STATE
    n_in = SSM_CHUNK * SSM_GROUP
    col = lambda i: colp_ref[0, :, i:i + 1]
    tau_l = lax.broadcasted_iota(jnp.int32, (1, n_in), 1) // SSM_GROUP
    row_is_re = lax.broadcasted_iota(jnp.int32, (n_state2, 1), 0) < SSM_STATE
    rep_lanes = (lax.broadcasted_iota(jnp.int32, (SSM_GROUP, n_in), 1) % SSM_GROUP
                 == lax.broadcasted_iota(jnp.int32, (SSM_GROUP, n_in), 0)).astype(F32)
    spread = lambda a: jnp.dot(jnp.concatenate([a, a], axis=0), rep_lanes, preferred_element_type=F32,
                               precision=lax.Precision.HIGHEST)
    ct_re = spread(ct_re_ref[0])
    ct_im = spread(ct_im_ref[0])
    pr, pi = _lam_pow(tau_l, [col(2 + 2 * b) for b in range(LAM_POWERS)],
                      [col(3 + 2 * b) for b in range(LAM_POWERS)], (n_state2, n_in))
    g_re, g_im = ct_re * pr - ct_im * pi, ct_re * pi + ct_im * pr
    rhs_k = jnp.where(row_is_re, g_re, g_im)
    q_re, q_im = g_re * col(2) - g_im * col(3), g_re * col(3) + g_im * col(2)
    w2_ref[0] = jnp.where(row_is_re, q_re, -q_im).astype(w2_ref.dtype)

    row = lambda i: rowp_ref[0, i:i + 1, :]
    lane_is_re = lax.broadcasted_iota(jnp.int32, (1, n_state2), 1) < SSM_STATE
    w_re, w_im = row(0), row(1)
    rep_halves = (lax.broadcasted_iota(jnp.int32, (SSM_STATE, n_state2), 1) % SSM_STATE
                  == lax.broadcasted_iota(jnp.int32, (SSM_STATE, n_state2), 0)).astype(F32)
    stack = lambda a: jnp.concatenate([jnp.dot(a, rep_halves, preferred_element_type=F32,
                                               precision=lax.Precision.HIGHEST)] * SSM_CHUNK, axis=0)
    bt_re = stack(bt_re_ref[0])
    bt_im = stack(bt_im_ref[0])
    bb_re = w_re * bt_re - w_im * bt_im
    bb_im = w_re * bt_im + w_im * bt_re

    lhs_k = jnp.where(lane_is_re, bb_re[0:SSM_GROUP], -bb_im[0:SSM_GROUP])
    kk = jnp.dot(lhs_k, rhs_k, preferred_element_type=F32, precision=lax.Precision.HIGHEST)
    lane_in = lax.broadcasted_iota(jnp.int32, (1, n_in), 1)
    for s in range(SSM_CHUNK):
        shifted = kk if s == 0 else pltpu.roll(kk, s * SSM_GROUP, 1)
        piece = jnp.where(lane_in >= s * SSM_GROUP, shifted, 0.0)
        w1_ref[0, s * SSM_GROUP:(s + 1) * SSM_GROUP, 0:n_in] = piece.astype(w1_ref.dtype)

    tau_row = SSM_CHUNK - 1 - lax.broadcasted_iota(jnp.int32, (n_in, 1), 0) // SSM_GROUP
    pr_r, pi_r = _lam_pow(tau_row, [row(2 + 2 * b) for b in range(LAM_POWERS)],
                          [row(3 + 2 * b) for b in range(LAM_POWERS)], (n_in, n_state2))
    p_re = bb_re * pr_r - bb_im * pi_r
    p_im = bb_re * pi_r + bb_im * pr_r
    w1_ref[0, :, n_in:n_in + n_state2] = jnp.where(lane_is_re, p_re, p_im).astype(w1_ref.dtype)
    w1_ref[0, :, n_in + n_state2:n_in + 2 * n_state2] = jnp.where(lane_is_re, p_im, p_re).astype(w1_ref.dtype)

    full, half = 2 * (LAM_POWERS - 1), 2 * (LAM_POWERS - 2)
    zero = jnp.zeros_like(w_re)
    coef_ref[0] = jnp.concatenate(
        [row(2 + full), jnp.where(lane_is_re, -row(3 + full), row(3 + full)),
         row(2 + half), jnp.where(lane_is_re, -row(3 + half), row(3 + half)), zero, zero, zero, zero], axis=0)


def _ssm_prep(log_dt, a_re, a_im, b_re, b_im, c_re, c_im):
    g, p = a_re.shape
    c = b_re.shape[-1]
    n_in = SSM_CHUNK * c
    n_tab = 2 + 2 * LAM_POWERS
    tab = pl.pallas_call(
        _ssm_lam_kernel,
        out_shape=jax.ShapeDtypeStruct((n_tab, g, p), F32),
        name="ssm_lam",
    )(log_dt.reshape(g, 1), a_re, a_im)
    tab2 = jnp.concatenate([tab, tab], axis=-1)
    pad = TAB_ROWS - n_tab
    cols = jnp.pad(tab2.transpose(1, 2, 0), ((0, 0), (0, 0), (0, pad)))
    rows = jnp.pad(tab2.transpose(1, 0, 2), ((0, 0), (0, pad), (0, 0)))
    ct = lambda a: jnp.swapaxes(a, 1, 2)
    bt = lambda a: jnp.swapaxes(a, 1, 2)
    spec3 = lambda s1, s2: pl.BlockSpec((1, s1, s2), lambda i: (i, 0, 0))
    w1, w2, coef = pl.pallas_call(
        _ssm_prep_kernel,
        grid=(g,),
        in_specs=[spec3(2 * p, TAB_ROWS), spec3(TAB_ROWS, 2 * p), spec3(p, c), spec3(p, c), spec3(c, p), spec3(c, p)],
        out_specs=[spec3(n_in, n_in + 4 * p), spec3(2 * p, n_in), spec3(8, 2 * p)],
        out_shape=[jax.ShapeDtypeStruct((g, n_in, n_in + 4 * p), BF16),
                   jax.ShapeDtypeStruct((g, 2 * p, n_in), BF16),
                   jax.ShapeDtypeStruct((g, 8, 2 * p), F32)],
        compiler_params=_params(("parallel",)),
        name="ssm_prep",
    )(cols, rows, ct(c_re), ct(c_im), bt(b_re), bt(b_im))
    return w1, w2, jnp.swapaxes(coef, 0, 1)[:4]


def _swap_array_and_lane_block(xs, lane_block_width):
    n = len(xs)
    lanes = n * lane_block_width
    blk = lax.broadcasted_iota(jnp.int32, (1, lanes), 1) // lane_block_width
    d = n // 2
    while d >= 1:
        keep = (blk & d) == 0
        new = list(xs)
        for i in range(n):
            if i & d == 0:
                lo, hi = xs[i], xs[i + d]
                new[i] = jnp.where(keep, lo, pltpu.roll(hi, d * lane_block_width, 1))
                new[i + d] = jnp.where(keep, pltpu.roll(lo, lanes - d * lane_block_width, 1), hi)
        xs = new
        d //= 2
    return xs


def _ssm_seq_kernel(u_ref, w1_ref, w2_ref, coef_ref, y_ref, hfin_ref, s_ref, ss_ref, yc_ref):
    gb = SSM_GROUP_BLOCK
    nk = u_ref.shape[0] // SSM_CHUNK
    n_in = SSM_CHUNK * SSM_GROUP
    n_state2 = 2 * SSM_STATE
    steps_per_tile = LANES // SSM_GROUP
    n_tiles = SSM_CHUNK // steps_per_tile
    assert gb * SSM_GROUP == LANES and steps_per_tile == gb

    u_cm = []
    for t in range(n_tiles):
        by_step = [u_ref[pl.ds(t * steps_per_tile + i, nk, stride=SSM_CHUNK), :] for i in range(steps_per_tile)]
        u_cm.append(_swap_array_and_lane_block(by_step, SSM_GROUP))
    for g in range(gb):
        u_g = jnp.concatenate([u_cm[t][g] for t in range(n_tiles)], axis=1).astype(BF16)
        r = jnp.dot(u_g, w1_ref[g], preferred_element_type=F32)
        yc_ref[g] = r[:, 0:n_in]
        s_ref[pl.ds(g, nk, stride=gb), :] = r[:, n_in:n_in + n_state2]
        ss_ref[pl.ds(g, nk, stride=gb), :] = r[:, n_in + n_state2:n_in + 2 * n_state2]
    a = coef_ref[0]
    b = coef_ref[1]

    def step(k, carry):
        x, xs = carry
        r0 = pl.multiple_of(k * gb, gb)
        add = s_ref[pl.ds(r0, gb), :]
        adds = ss_ref[pl.ds(r0, gb), :]
        s_ref[pl.ds(r0, gb), :] = x
        return a * x + b * xs + add, a * xs - b * x + adds

    zero = jnp.zeros((gb, n_state2), F32)
    x_fin, _ = lax.fori_loop(0, nk, step, (zero, zero))
    hfin_ref[0] = x_fin
    for g in range(gb):
        h_prev = s_ref[pl.ds(g, nk, stride=gb), :]
        yc_ref[g] += jnp.dot(h_prev.astype(BF16), w2_ref[g], preferred_element_type=F32)
    for t in range(n_tiles):
        by_group = [yc_ref[g, :, t * LANES:(t + 1) * LANES] for g in range(gb)]
        by_step = _swap_array_and_lane_block(by_group, SSM_GROUP)
        for i in range(steps_per_tile):
            y_ref[pl.ds(t * steps_per_tile + i, nk, stride=SSM_CHUNK), :] = by_step[i]


def _ssm_seq(u, w1, w2, coef, n_seq):
    m, width = u.shape
    seq = m // n_seq
    g = w1.shape[0]
    gb = SSM_GROUP_BLOCK
    nk = seq // SSM_CHUNK
    n_in = SSM_CHUNK * SSM_GROUP
    n_state2 = 2 * SSM_STATE
    tok = pl.BlockSpec((seq, LANES), lambda i, j: (i, j))
    return pl.pallas_call(
        _ssm_seq_kernel,
        grid=(n_seq, g // gb),
        in_specs=[tok,
                  pl.BlockSpec((gb, n_in, n_in + 2 * n_state2), lambda i, j: (j, 0, 0)),
                  pl.BlockSpec((gb, n_state2, n_in), lambda i, j: (j, 0, 0)),
                  pl.BlockSpec((4, gb, n_state2), lambda i, j: (0, j, 0))],
        out_specs=[tok, pl.BlockSpec((1, gb, n_state2), lambda i, j: (i, j, 0))],
        out_shape=[jax.ShapeDtypeStruct((m, width), F32),
                   jax.ShapeDtypeStruct((n_seq, g, n_state2), F32)],
        scratch_shapes=[pltpu.VMEM((nk * gb, n_state2), F32), pltpu.VMEM((nk * gb, n_state2), F32),
                        pltpu.VMEM((gb, nk, n_in), F32)],
        compiler_params=_params(("parallel", "parallel")),
        name="ssm_seq",
    )(u, w1, w2, coef)


def _ssm_step_kernel(u_ref, h0_ref, h0s_ref, w1_ref, w2_ref, coef_ref, y_ref, hfin_ref):
    gb = u_ref.shape[0]
    n_half = u_ref.shape[2]
    n_in = 2 * n_half
    n_state2 = 2 * SSM_STATE
    for g in range(gb):
        u = u_ref[g]
        h0 = h0_ref[g]
        t_half = w1_ref[g, 0:n_half, 0:n_half]
        p_half = w1_ref[g, n_half:n_in, n_in:n_in + n_state2]
        q_half = w2_ref[g, :, 0:n_half]
        y_ref[g] = (jnp.dot(u, t_half, preferred_element_type=F32)
                    + jnp.dot(h0.astype(BF16), q_half, preferred_element_type=F32))
        hfin_ref[g] = (coef_ref[2, g:g + 1, :] * h0 + coef_ref[3, g:g + 1, :] * h0s_ref[g]
                       + jnp.dot(u, p_half, preferred_element_type=F32))


def _ssm_step(u8, h0, h0s, w1, w2, coef):
    g, n, n_half = u8.shape
    gb = SSM_GROUP_BLOCK
    n_state2 = 2 * SSM_STATE
    n_in = 2 * n_half
    blk = lambda s1, s2: pl.BlockSpec((gb, s1, s2), lambda j: (j, 0, 0))
    return pl.pallas_call(
        _ssm_step_kernel,
        grid=(g // gb,),
        in_specs=[blk(n, n_half), blk(n, n_state2), blk(n, n_state2),
                  blk(n_in, n_in + 2 * n_state2), blk(n_state2, n_in),
                  pl.BlockSpec((4, gb, n_state2), lambda j: (0, j, 0))],
        out_specs=[blk(n, n_half), blk(n, n_state2)],
        out_shape=[jax.ShapeDtypeStruct((g, n, n_half), F32), jax.ShapeDtypeStruct((g, n, n_state2), F32)],
        compiler_params=_params(("parallel",)),
        name="ssm_step",
    )(u8, h0, h0s, w1, w2, coef)


def _s5_post_kernel(y_ref, u_ref, z_ref, x_ref, d_ref, wg_ref, bg_ref, wo_ref, gp_ref, o_ref):
    y = jax.nn.gelu(y_ref[...] + d_ref[...] * u_ref[...])
    gate = jnp.dot(y.astype(BF16), wg_ref[...], preferred_element_type=F32) + bg_ref[...]
    y = y * jax.nn.sigmoid(gate)
    z = z_ref[...].astype(F32)
    y = y * (z * jax.nn.sigmoid(z))
    o = jnp.dot(y.astype(BF16), wo_ref[...], preferred_element_type=F32)
    o_ref[...] = x_ref[...] + (o * _rms_scale(o)) * gp_ref[...]


def _s5_post(y, u, z, x, d_skip, w_glu, b_glu, w_out, g_post, block_rows):
    m, d = x.shape
    tm = min(block_rows, m)
    row = pl.BlockSpec((tm, d), lambda i: (i, 0))
    vec = pl.BlockSpec((1, d), lambda i: (0, 0))
    mat = pl.BlockSpec((d, d), lambda i: (0, 0))
    return pl.pallas_call(
        _s5_post_kernel,
        grid=(m // tm,),
        in_specs=[row, row, row, row, vec, mat, vec, mat, vec],
        out_specs=row,
        out_shape=jax.ShapeDtypeStruct((m, d), F32),
        compiler_params=_params(("parallel",)),
        name="s5_post",
    )(y, u, z, x, d_skip.reshape(1, d), w_glu, b_glu.reshape(1, d), w_out, g_post.reshape(1, d))


def _sb_post_kernel(o_ref, g_ref, x_ref, wo_ref, gp_ref, out_ref):
    g = g_ref[...].astype(F32)
    y = o_ref[...].astype(F32) * (g * jax.nn.sigmoid(g))
    r = jnp.dot(y.astype(BF16), wo_ref[...], preferred_element_type=F32)
    out_ref[...] = x_ref[...] + (r * _rms_scale(r)) * gp_ref[...]


def _sb_post(o, g, x, w_out, g_post, block_rows):
    m, d = x.shape
    tm = min(block_rows, m)
    row = pl.BlockSpec((tm, d), lambda i: (i, 0))
    return pl.pallas_call(
        _sb_post_kernel,
        grid=(m // tm,),
        in_specs=[row, row, row, pl.BlockSpec((d, d), lambda i: (0, 0)), pl.BlockSpec((1, d), lambda i: (0, 0))],
        out_specs=row,
        out_shape=jax.ShapeDtypeStruct((m, d), F32),
        compiler_params=_params(("parallel",)),
        name="sb_post",
    )(o, g, x, w_out, g_post.reshape(1, d))


NEG_LOG2_E = -1.4426950408889634


def _suffix_mask(n):
    j = lax.broadcasted_iota(jnp.int32, (2 * n, n), 0)
    s = lax.broadcasted_iota(jnp.int32, (2 * n, n), 1)
    return ((j >= s) & ((j < n) | (j >= s + n))).astype(BF16)


def _miss(z, mask):
    sp = jnp.maximum(z, 0.0) + jnp.log(1.0 + jnp.exp2(jnp.abs(z) * NEG_LOG2_E))
    return sp if mask is None else jnp.where(mask, sp, 0.0)


def _suffix_sums(miss, tri):
    hi = miss.astype(BF16)
    lo = (miss - hi.astype(F32)).astype(BF16)
    return jnp.dot(jnp.concatenate([hi, lo], axis=1), tri, preferred_element_type=F32)


def _stick_weights(z, misses_from_here, mask):
    w = jnp.exp(z - misses_from_here)
    return w if mask is None else jnp.where(mask, w, 0.0)


def _attn_prompt_kernel(bias_ref, q_ref, kt_ref, vt_ref, o_ref, acc_ref, later_ref):
    tb = ATT_BLOCK
    hp = pl.program_id(1)
    qi = pl.program_id(2)
    q = q_ref[0]
    lane = lax.broadcasted_iota(jnp.int32, (1, LANES), 1)
    tri = _suffix_mask(tb)
    diag = lax.broadcasted_iota(jnp.int32, (tb, tb), 1) < lax.broadcasted_iota(jnp.int32, (tb, tb), 0)
    heads = range(HEADS_PER_LANE_TILE)
    q_aug = []
    for hh in heads:
        qm = jnp.where((lane >= hh * HEAD_DIM) & (lane < (hh + 1) * HEAD_DIM), q, jnp.zeros_like(q))
        rest = jnp.full((1, LANES), bias_ref[hp * HEADS_PER_LANE_TILE + hh], F32)
        terms = jnp.zeros((1, LANES), F32)
        for t in range(BIAS_TERMS):
            term = rest.astype(BF16).astype(F32)
            terms = jnp.where(lane == t, term, terms)
            rest = rest - term
        q_aug.append(jnp.concatenate([qm, jnp.broadcast_to(terms, qm.shape).astype(BF16)], axis=1))
    ones_rows = (lax.broadcasted_iota(jnp.int32, (LANES, tb), 0) < BIAS_TERMS).astype(BF16)
    neg_tri = -tri
    acc_ref[...] = jnp.zeros_like(acc_ref)
    later_ref[...] = jnp.zeros_like(later_ref)

    def visit(kj, subs_masks):
        kb = jnp.concatenate([kt_ref[0, 0, kj], ones_rows], axis=0)
        kb_minus_tri = jnp.concatenate([kb, neg_tri], axis=0)
        vb = vt_ref[0, 0, kj]
        chains = [(hh, slice(sub * tb, (sub + 1) * tb), mask) for hh in heads for sub, mask in subs_masks]
        zs = [jnp.dot(q_aug[hh][rows], kb, preferred_element_type=F32) for hh, rows, _ in chains]
        hi_lo, z0 = [], []
        for z, (_, _, mask) in zip(zs, chains):
            miss = _miss(z, mask)
            hi = miss.astype(BF16)
            hi_lo.append(jnp.concatenate([hi, (miss - hi.astype(F32)).astype(BF16)], axis=1))
            z0.append(z[:, 0:1])
        log_ws = [jnp.dot(jnp.concatenate([q_aug[hh][rows], hl], axis=1), kb_minus_tri, preferred_element_type=F32)
                  for hl, (hh, rows, _) in zip(hi_lo, chains)]
        for lw, z_first, (hh, rows, mask) in zip(log_ws, z0, chains):
            c = later_ref[hh, rows]
            w = jnp.exp(lw - c)
            w = (w if mask is None else jnp.where(mask, w, 0.0)).astype(BF16)
            acc_ref[hh, rows] += lax.dot_general(w, vb, (((1,), (1,)), ((), ())), preferred_element_type=F32)
            later_ref[hh, rows] = c + (z_first - lw[:, 0:1])

    first = qi * Q_SUBS
    for lead in reversed(range(Q_SUBS)):
        visit(first + lead, [(lead, diag)] + [(sub, None) for sub in range(lead + 1, Q_SUBS)])

    def body(t, carry):
        visit(first - 1 - t, [(sub, None) for sub in range(Q_SUBS)])
        return carry

    lax.fori_loop(0, first, body, 0)
    o_ref[0] = jnp.where(lane < HEAD_DIM, acc_ref[0], acc_ref[1]).astype(o_ref.dtype)


def _attn_prompt(q, kt, vt, bias):
    b, s, hd = q.shape
    n_kb = kt.shape[2]
    tq = Q_SUBS * ATT_BLOCK
    qspec = pl.BlockSpec((1, tq, LANES), lambda i, h, j: (i, j, h))
    kspec = pl.BlockSpec((1, 1, n_kb, LANES, ATT_BLOCK), lambda i, h, j: (i, h, 0, 0, 0))
    return pl.pallas_call(
        _attn_prompt_kernel,
        grid=(b, hd // LANES, s // tq),
        in_specs=[pl.BlockSpec(memory_space=pltpu.SMEM), qspec, kspec, kspec],
        out_specs=qspec,
        out_shape=jax.ShapeDtypeStruct((b, s, hd), BF16),
        scratch_shapes=[pltpu.VMEM((HEADS_PER_LANE_TILE, tq, LANES), F32),
                        pltpu.VMEM((HEADS_PER_LANE_TILE, tq, 1), F32)],
        compiler_params=_params(("parallel", "parallel", "arbitrary")),
        name="attn_prompt",
    )(bias, q, kt, vt)


def _attn_paged_kernel(pt_ref, q_ref, knew_ref, vnew_ref, *rest, pages_per_step):
    del pt_ref
    kpage_refs = rest[:pages_per_step]
    vpage_refs = rest[pages_per_step:2 * pages_per_step]
    bias_ref, o_ref, qbd_ref, acc_ref, later_ref = rest[2 * pages_per_step:]
    j = pl.program_id(1)
    nq = q_ref.shape[0]
    hd = q_ref.shape[1]
    n_heads = hd // HEAD_DIM
    rows = n_heads * nq
    tk = knew_ref.shape[1]
    row = lax.broadcasted_iota(jnp.int32, (rows, 1), 0)
    tri = _suffix_mask(tk)
    contract_last = (((1,), (1,)), ((), ()))

    @pl.when(j == 0)
    def _():
        lane = lax.broadcasted_iota(jnp.int32, (1, hd), 1)
        q_rep = jnp.concatenate([q_ref[...]] * n_heads, axis=0)
        qbd = jnp.where(lane // HEAD_DIM == row // nq, q_rep, 0.0).astype(BF16)
        qbd_ref[...] = qbd
        col = lax.broadcasted_iota(jnp.int32, (1, tk), 1)
        mask = col < row % nq
        z = lax.dot_general(qbd, knew_ref[0].astype(BF16), contract_last, preferred_element_type=F32) + bias_ref[...]
        sums = _suffix_sums(_miss(z, mask), tri)
        w = _stick_weights(z, sums, mask).astype(BF16)
        acc_ref[...] = jnp.dot(w, vnew_ref[0].astype(BF16), preferred_element_type=F32)
        later_ref[...] = sums[:, 0:1]

    @pl.when(j > 0)
    def _():
        order = list(reversed(range(pages_per_step)))
        zs = [jnp.dot(qbd_ref[...], kpage_refs[c][0].reshape(hd, tk).astype(BF16), preferred_element_type=F32)
              + bias_ref[...] for c in order]
        sums = [_suffix_sums(_miss(z, None), tri) for z in zs]
        later_blocks = later_ref[...]
        ws = []
        for z, sm in zip(zs, sums):
            ws.append(_stick_weights(z, sm + later_blocks, None).astype(BF16))
            later_blocks = later_blocks + sm[:, 0:1]
        later_ref[...] = later_blocks
        v_all = jnp.concatenate([vpage_refs[c][0].reshape(hd, tk).astype(BF16) for c in order], axis=1)
        acc_ref[...] += lax.dot_general(jnp.concatenate(ws, axis=1), v_all, contract_last,
                                        preferred_element_type=F32)

    @pl.when(j == pl.num_programs(1) - 1)
    def _():
        lane = lax.broadcasted_iota(jnp.int32, (1, hd), 1)
        out = jnp.zeros((nq, hd), F32)
        for h in range(n_heads):
            out = out + jnp.where(lane // HEAD_DIM == h, acc_ref[h * nq:(h + 1) * nq, :], 0.0)
        o_ref[...] = out


def _attn_paged(q, k_new, v_new, cache_kt, cache_vt, page_table, bias, n_new):
    n_seq, n_pages = page_table.shape
    hd = q.shape[1]
    n_heads = hd // HEAD_DIM
    rows = n_heads * n_new
    page = cache_kt.shape[3]
    pps = PAGES_PER_STEP
    seq_rows = pl.BlockSpec((n_new, hd), lambda n, j, pt: (n, 0))
    new_spec = pl.BlockSpec((1, page, hd), lambda n, j, pt: (n, 0, 0))

    def page_spec(c):
        return pl.BlockSpec((1, n_heads, HEAD_DIM, page),
                            lambda n, j, pt: (pt[n, n_pages - jnp.maximum(j, 1) * pps + c], 0, 0, 0))

    grid_spec = pltpu.PrefetchScalarGridSpec(
        num_scalar_prefetch=1,
        grid=(n_seq, n_pages // pps + 1),
        in_specs=([seq_rows, new_spec, new_spec] + [page_spec(c) for c in range(pps)] * 2
                  + [pl.BlockSpec((rows, 1), lambda n, j, pt: (0, 0))]),
        out_specs=seq_rows,
        scratch_shapes=[pltpu.VMEM((rows, hd), BF16), pltpu.VMEM((rows, hd), F32), pltpu.VMEM((rows, 1), F32)],
    )
    return pl.pallas_call(
        functools.partial(_attn_paged_kernel, pages_per_step=pps),
        grid_spec=grid_spec,
        out_shape=jax.ShapeDtypeStruct((n_seq * n_new, hd), F32),
        compiler_params=_params(("parallel", "arbitrary")),
        name="attn_paged",
    )(page_table, q, k_new, v_new, *([cache_kt] * pps), *([cache_vt] * pps),
      jnp.repeat(bias, n_new).reshape(rows, 1))


def _s5_layer(xp, xs, n_prompt, h0_re, h0_im, prm):
    (g_pre, g_post, w_in, log_dt, a_re, a_im, b_re, b_im, c_re, c_im, d_skip, w_glu, b_glu, w_out) = prm
    d = xp.shape[1]
    n_groups, n_state = a_re.shape
    gc = b_re.shape[-1]
    width = n_groups * gc
    w1, w2, coef = _ssm_prep(log_dt, a_re, a_im, b_re, b_im, c_re, c_im)
    w_in_b, w_glu_b, w_out_b = w_in.astype(BF16), w_glu.astype(BF16), w_out.astype(BF16)
    uz_outs = [(0, width, 1.0, F32), (width, 2 * width, 1.0, BF16)]

    u, z = _norm_matmul(xp, g_pre, w_in_b, uz_outs, 512)
    y, hfin = _ssm_seq(u, w1, w2, coef, n_prompt)
    xp = _s5_post(y, u, z, xp, d_skip, w_glu_b, b_glu, w_out_b, g_post, 512)

    n_seq = h0_re.shape[0]
    t_new = xs.shape[0] // n_seq
    us, zs = _norm_matmul(xs, g_pre, w_in_b, uz_outs, 512)
    u8 = (us.reshape(n_seq, t_new, n_groups, gc).transpose(2, 0, 1, 3)
          .reshape(n_groups, n_seq, t_new * gc).astype(BF16))
    h0 = jnp.concatenate([h0_re, h0_im], axis=-1).transpose(1, 0, 2)
    h0s = jnp.concatenate([h0_im, h0_re], axis=-1).transpose(1, 0, 2)
    y8, hf8 = _ssm_step(u8, h0, h0s, w1, w2, coef)
    ys = y8.reshape(n_groups, n_seq, t_new, gc).transpose(1, 2, 0, 3).reshape(xs.shape[0], width)
    xs = _s5_post(ys, us, zs, xs, d_skip, w_glu_b, b_glu, w_out_b, g_post, 512)
    hf8 = hf8.transpose(1, 0, 2)
    return (xp, xs, hfin[..., :n_state], hfin[..., n_state:], hf8[..., :n_state], hf8[..., n_state:])


def kernel(x_prompt, x_sample, state_ssm_re, state_ssm_im, cache_k, cache_v, page_table, a_norm_pre, a_norm_post, a_w_in, a_log_dt, a_A_re, a_A_im, a_B_re, a_B_im, a_C_re, a_C_im, a_D, a_w_glu, a_b_glu, a_w_out, kv_norm, w_kv, b_norm_pre, b_norm_post, b_w_in, b_logit_bias, b_w_out):
    n_prompt, seq, d = x_prompt.shape
    n_seq, t_new, _ = x_sample.shape
    n_a = a_w_in.shape[0]
    n_b = b_w_in.shape[0]
    att = w_kv.shape[1] // 2
    n_heads = att // HEAD_DIM
    assert t_new * 2 == SSM_CHUNK and cache_k.shape[1] == PAGE_SIZE and HEAD_DIM ** -0.5 == 0.125

    xp = x_prompt.reshape(n_prompt * seq, d)
    xs = x_sample.reshape(n_seq * t_new, d)
    p_re, p_im, s_re, s_im = [], [], [], []
    for i in range(n_a):
        prm = (a_norm_pre[i], a_norm_post[i], a_w_in[i], a_log_dt[i], a_A_re[i], a_A_im[i], a_B_re[i], a_B_im[i],
               a_C_re[i], a_C_im[i], a_D[i], a_w_glu[i], a_b_glu[i], a_w_out[i])
        xp, xs, hr, hi, sr, si = _s5_layer(xp, xs, n_prompt, state_ssm_re[i], state_ssm_im[i], prm)
        p_re.append(hr)
        p_im.append(hi)
        s_re.append(sr)
        s_im.append(si)

    w_kv_b = w_kv.astype(BF16)
    kt_p, vt_p, kt_pb, vt_pb = _kv_proj(xp, kv_norm, w_kv_b.T, n_prompt, 512)
    k_p = kt_p.reshape(n_prompt, n_heads, HEAD_DIM, seq).transpose(0, 3, 1, 2)
    v_p = vt_p.reshape(n_prompt, n_heads, HEAD_DIM, seq).transpose(0, 3, 1, 2)
    k_s, v_s = _norm_matmul(xs, kv_norm, w_kv_b, [(0, att, 1.0, F32), (att, 2 * att, 1.0, F32)], 512)
    pad_new = lambda a: jnp.pad(a.reshape(n_seq, t_new, att), ((0, 0), (0, PAGE_SIZE - t_new), (0, 0)))
    k_s_pad, v_s_pad = pad_new(k_s), pad_new(v_s)
    cache_kt = cache_k.transpose(0, 2, 3, 1)
    cache_vt = cache_v.transpose(0, 2, 3, 1)

    scale = HEAD_DIM ** -0.5
    for j in range(n_b):
        w_in_b, w_out_b = b_w_in[j].astype(BF16), b_w_out[j].astype(BF16)
        q, g = _norm_matmul(xp, b_norm_pre[j], w_in_b, [(0, att, scale, BF16), (att, 2 * att, 1.0, BF16)], 512)
        o = _attn_prompt(q.reshape(n_prompt, seq, att), kt_pb, vt_pb, b_logit_bias[j])
        xp = _sb_post(o.reshape(n_prompt * seq, att), g, xp, w_out_b, b_norm_post[j], 512)
        qs, gs = _norm_matmul(xs, b_norm_pre[j], w_in_b, [(0, att, scale, F32), (att, 2 * att, 1.0, F32)], 512)
        os_ = _attn_paged(qs, k_s_pad, v_s_pad, cache_kt, cache_vt, page_table, b_logit_bias[j], t_new)
        xs = _sb_post(os_, gs, xs, w_out_b, b_norm_post[j], 512)

    return (xp.reshape(n_prompt, seq, d), xs.reshape(n_seq, t_new, d),
            jnp.stack(p_re), jnp.stack(p_im),
            k_p, v_p,
            jnp.stack(s_re), jnp.stack(s_im),
            k_s.reshape(n_seq, t_new, n_heads, HEAD_DIM), v_s.reshape(n_seq, t_new, n_heads, HEAD_DIM))
```

```python
import functools

import jax
import jax.numpy as jnp
from jax import lax
from jax.experimental import pallas as pl
from jax.experimental.pallas import tpu as pltpu

F32 = jnp.float32
BF16 = jnp.bfloat16

RMS_EPS = 1e-6
SSM_GROUP = 16
SSM_STATE = 64
SSM_CHUNK = 16
SSM_GROUP_BLOCK = 8
LAM_POWERS = 5
TAB_ROWS = 16
HEAD_DIM = 64
LANES = 128
HEADS_PER_LANE_TILE = LANES // HEAD_DIM
ATT_BLOCK = 256
Q_SUBS = 4
LOOP_BLOCKS = 4
BIAS_TERMS = 3
PAGE_SIZE = 128
PAGES_PER_STEP = 8
VMEM_LIMIT = 48 * 1024 * 1024


def _params(semantics):
    return pltpu.CompilerParams(dimension_semantics=semantics, vmem_limit_bytes=VMEM_LIMIT)


def _rms_scale(x):
    return lax.rsqrt(jnp.mean(x * x, axis=-1, keepdims=True) + RMS_EPS)


def _norm_matmul_kernel(x_ref, g_ref, w_ref, *o_refs, splits):
    x = x_ref[...]
    h = (x * _rms_scale(x)) * g_ref[...]
    acc = jnp.dot(h.astype(BF16), w_ref[...], preferred_element_type=F32)
    for o_ref, (lo, hi, scale) in zip(o_refs, splits):
        v = acc[:, lo:hi]
        if scale != 1.0:
            v = v * scale
        o_ref[...] = v.astype(o_ref.dtype)


def _norm_matmul(x, g, w, outs, block_rows):
    m, d = x.shape
    n = w.shape[1]
    tm = min(block_rows, m)
    splits = tuple((lo, hi, scale) for lo, hi, scale, _ in outs)
    return pl.pallas_call(
        functools.partial(_norm_matmul_kernel, splits=splits),
        grid=(m // tm,),
        in_specs=[pl.BlockSpec((tm, d), lambda i: (i, 0)),
                  pl.BlockSpec((1, d), lambda i: (0, 0)),
                  pl.BlockSpec((d, n), lambda i: (0, 0))],
        out_specs=[pl.BlockSpec((tm, hi - lo), lambda i: (i, 0)) for lo, hi, _, _ in outs],
        out_shape=[jax.ShapeDtypeStruct((m, hi - lo), dt) for lo, hi, _, dt in outs],
        compiler_params=_params(("parallel",)),
        name="norm_matmul",
    )(x, g.reshape(1, d), w)


def _kv_proj_kernel(x_ref, g_ref, wt_ref, kt_ref, vt_ref, ktb_ref, vtb_ref):
    x = x_ref[...]
    h = (x * _rms_scale(x)) * g_ref[...]
    acc = lax.dot_general(wt_ref[...], h.astype(BF16), (((1,), (1,)), ((), ())), preferred_element_type=F32)
    att = kt_ref.shape[1]
    n_blocks = ktb_ref.shape[2]
    for t_ref, b_ref, lo in ((kt_ref, ktb_ref, 0), (vt_ref, vtb_ref, att)):
        part = acc[lo:lo + att]
        t_ref[0] = part
        for c in range(n_blocks):
            blk = part[:, c * ATT_BLOCK:(c + 1) * ATT_BLOCK]
            b_ref[0, :, c] = blk.reshape(att // LANES, LANES, ATT_BLOCK).astype(b_ref.dtype)


def _kv_proj(x, g, wt, n_prompt, block_rows):
    m, d = x.shape
    att = wt.shape[0] // 2
    seq = m // n_prompt
    tm = min(block_rows, seq)
    steps = seq // tm
    n_blocks = tm // ATT_BLOCK
    t_spec = pl.BlockSpec((1, att, tm), lambda b, i: (b, 0, i))
    b_spec = pl.BlockSpec((1, att // LANES, n_blocks, LANES, ATT_BLOCK), lambda b, i: (b, 0, i, 0, 0))
    blocked = jax.ShapeDtypeStruct((n_prompt, att // LANES, seq // ATT_BLOCK, LANES, ATT_BLOCK), BF16)
    return pl.pallas_call(
        _kv_proj_kernel,
        grid=(n_prompt, steps),
        in_specs=[pl.BlockSpec((tm, d), lambda b, i: (b * steps + i, 0)),
                  pl.BlockSpec((1, d), lambda b, i: (0, 0)),
                  pl.BlockSpec((2 * att, d), lambda b, i: (0, 0))],
        out_specs=[t_spec, t_spec, b_spec, b_spec],
        out_shape=[jax.ShapeDtypeStruct((n_prompt, att, seq), F32)] * 2 + [blocked] * 2,
        compiler_params=_params(("parallel", "parallel")),
        name="kv_proj",
    )(x, g.reshape(1, d), wt)


def _ssm_lam_kernel(log_dt_ref, a_re_ref, a_im_ref, tab_ref):
    a_re = a_re_ref[...]
    a_im = a_im_ref[...]
    dt = jnp.exp(log_dt_ref[...])
    mag = jnp.exp(dt * a_re)
    lr = mag * jnp.cos(dt * a_im)
    li = mag * jnp.sin(dt * a_im)
    den = a_re * a_re + a_im * a_im
    tab_ref[0] = ((lr - 1.0) * a_re + li * a_im) / den
    tab_ref[1] = (li * a_re - (lr - 1.0) * a_im) / den
    for b in range(LAM_POWERS):
        tab_ref[2 + 2 * b] = lr
        tab_ref[3 + 2 * b] = li
        lr, li = lr * lr - li * li, 2.0 * (lr * li)


def _lam_pow(tau, pow_re, pow_im, shape):
    pr = pi = None
    for b in range(LAM_POWERS - 1):
        on = ((tau >> b) & 1) == 1
        fr = jnp.where(on, jnp.broadcast_to(pow_re[b], shape), 1.0)
        fi = jnp.where(on, jnp.broadcast_to(pow_im[b], shape), 0.0)
        pr, pi = (fr, fi) if pr is None else (pr * fr - pi * fi, pr * fi + pi * fr)
    return pr, pi


def _ssm_prep_kernel(colp_ref, rowp_ref, ct_re_ref, ct_im_ref, bt_re_ref, bt_im_ref,
                     w1_ref, w2_ref, coef_ref):
    n_state2 = 2 * SSM_STATE
    n_in = SSM_CHUNK * SSM_GROUP
    col = lambda i: colp_ref[0, :, i:i + 1]
    tau_l = lax.broadcasted_iota(jnp.int32, (1, n_in), 1) // SSM_GROUP
    row_is_re = lax.broadcasted_iota(jnp.int32, (n_state2, 1), 0) < SSM_STATE
    rep_lanes = (lax.broadcasted_iota(jnp.int32, (SSM_GROUP, n_in), 1) % SSM_GROUP
                 == lax.broadcasted_iota(jnp.int32, (SSM_GROUP, n_in), 0)).astype(F32)
    spread = lambda a: jnp.dot(jnp.concatenate([a, a], axis=0), rep_lanes, preferred_element_type=F32,
                               precision=lax.Precision.HIGHEST)
    ct_re = spread(ct_re_ref[0])
    ct_im = spread(ct_im_ref[0])
    pr, pi = _lam_pow(tau_l, [col(2 + 2 * b) for b in range(LAM_POWERS)],
                      [col(3 + 2 * b) for b in range(LAM_POWERS)], (n_state2, n_in))
    g_re, g_im = ct_re * pr - ct_im * pi, ct_re * pi + ct_im * pr
    rhs_k = jnp.where(row_is_re, g_re, g_im)
    q_re, q_im = g_re * col(2) - g_im * col(3), g_re * col(3) + g_im * col(2)
    w2_ref[0] = jnp.where(row_is_re, q_re, -q_im).astype(w2_ref.dtype)

    row = lambda i: rowp_ref[0, i:i + 1, :]
    lane_is_re = lax.broadcasted_iota(jnp.int32, (1, n_state2), 1) < SSM_STATE
    w_re, w_im = row(0), row(1)
    rep_halves = (lax.broadcasted_iota(jnp.int32, (SSM_STATE, n_state2), 1) % SSM_STATE
                  == lax.broadcasted_iota(jnp.int32, (SSM_STATE, n_state2), 0)).astype(F32)
    stack = lambda a: jnp.concatenate([jnp.dot(a, rep_halves, preferred_element_type=F32,
                                               precision=lax.Precision.HIGHEST)] * SSM_CHUNK, axis=0)
    bt_re = stack(bt_re_ref[0])
    bt_im = stack(bt_im_ref[0])
    bb_re = w_re * bt_re - w_im * bt_im
    bb_im = w_re * bt_im + w_im * bt_re

    lhs_k = jnp.where(lane_is_re, bb_re[0:SSM_GROUP], -bb_im[0:SSM_GROUP])
    kk = jnp.dot(lhs_k, rhs_k, preferred_element_type=F32, precision=lax.Precision.HIGHEST)
    lane_in = lax.broadcasted_iota(jnp.int32, (1, n_in), 1)
    for s in range(SSM_CHUNK):
        shifted = kk if s == 0 else pltpu.roll(kk, s * SSM_GROUP, 1)
        piece = jnp.where(lane_in >= s * SSM_GROUP, shifted, 0.0)
        w1_ref[0, s * SSM_GROUP:(s + 1) * SSM_GROUP, 0:n_in] = piece.astype(w1_ref.dtype)

    tau_row = SSM_CHUNK - 1 - lax.broadcasted_iota(jnp.int32, (n_in, 1), 0) // SSM_GROUP
    pr_r, pi_r = _lam_pow(tau_row, [row(2 + 2 * b) for b in range(LAM_POWERS)],
                          [row(3 + 2 * b) for b in range(LAM_POWERS)], (n_in, n_state2))
    p_re = bb_re * pr_r - bb_im * pi_r
    p_im = bb_re * pi_r + bb_im * pr_r
    w1_ref[0, :, n_in:n_in + n_state2] = jnp.where(lane_is_re, p_re, p_im).astype(w1_ref.dtype)
    w1_ref[0, :, n_in + n_state2:n_in + 2 * n_state2] = jnp.where(lane_is_re, p_im, p_re).astype(w1_ref.dtype)

    full, half = 2 * (LAM_POWERS - 1), 2 * (LAM_POWERS - 2)
    zero = jnp.zeros_like(w_re)
    coef_ref[0] = jnp.concatenate(
        [row(2 + full), jnp.where(lane_is_re, -row(3 + full), row(3 + full)),
         row(2 + half), jnp.where(lane_is_re, -row(3 + half), row(3 + half)), zero, zero, zero, zero], axis=0)


def _ssm_prep(log_dt, a_re, a_im, b_re, b_im, c_re, c_im):
    g, p = a_re.shape
    c = b_re.shape[-1]
    n_in = SSM_CHUNK * c
    n_tab = 2 + 2 * LAM_POWERS
    tab = pl.pallas_call(
        _ssm_lam_kernel,
        out_shape=jax.ShapeDtypeStruct((n_tab, g, p), F32),
        name="ssm_lam",
    )(log_dt.reshape(g, 1), a_re, a_im)
    tab2 = jnp.concatenate([tab, tab], axis=-1)
    pad = TAB_ROWS - n_tab
    cols = jnp.pad(tab2.transpose(1, 2, 0), ((0, 0), (0, 0), (0, pad)))
    rows = jnp.pad(tab2.transpose(1, 0, 2), ((0, 0), (0, pad), (0, 0)))
    ct = lambda a: jnp.swapaxes(a, 1, 2)
    bt = lambda a: jnp.swapaxes(a, 1, 2)
    spec3 = lambda s1, s2: pl.BlockSpec((1, s1, s2), lambda i: (i, 0, 0))
    w1, w2, coef = pl.pallas_call(
        _ssm_prep_kernel,
        grid=(g,),
        in_specs=[spec3(2 * p, TAB_ROWS), spec3(TAB_ROWS, 2 * p), spec3(p, c), spec3(p, c), spec3(c, p), spec3(c, p)],
        out_specs=[spec3(n_in, n_in + 4 * p), spec3(2 * p, n_in), spec3(8, 2 * p)],
        out_shape=[jax.ShapeDtypeStruct((g, n_in, n_in + 4 * p), BF16),
                   jax.ShapeDtypeStruct((g, 2 * p, n_in), BF16),
                   jax.ShapeDtypeStruct((g, 8, 2 * p), F32)],
        compiler_params=_params(("parallel",)),
        name="ssm_prep",
    )(cols, rows, ct(c_re), ct(c_im), bt(b_re), bt(b_im))
    return w1, w2, jnp.swapaxes(coef, 0, 1)[:4]


def _swap_array_and_lane_block(xs, lane_block_width):
    n = len(xs)
    lanes = n * lane_block_width
    blk = lax.broadcasted_iota(jnp.int32, (1, lanes), 1) // lane_block_width
    d = n // 2
    while d >= 1:
        keep = (blk & d) == 0
        new = list(xs)
        for i in range(n):
            if i & d == 0:
                lo, hi = xs[i], xs[i + d]
                if 2 * d == n:
                    moved = pltpu.roll(jnp.where(keep, hi, lo), d * lane_block_width, 1)
                    new[i] = jnp.where(keep, lo, moved)
                    new[i + d] = jnp.where(keep, moved, hi)
                else:
                    new[i] = jnp.where(keep, lo, pltpu.roll(hi, d * lane_block_width, 1))
                    new[i + d] = jnp.where(keep, pltpu.roll(lo, lanes - d * lane_block_width, 1), hi)
        xs = new
        d //= 2
    return xs


def _ssm_seq_kernel(u_ref, w1_ref, w2_ref, coef_ref, y_ref, hfin_ref, s_ref, ss_ref, yc_ref):
    gb = SSM_GROUP_BLOCK
    nk = u_ref.shape[0] // SSM_CHUNK
    n_in = SSM_CHUNK * SSM_GROUP
    n_state2 = 2 * SSM_STATE
    steps_per_tile = LANES // SSM_GROUP
    n_tiles = SSM_CHUNK // steps_per_tile
    assert gb * SSM_GROUP == LANES and steps_per_tile == gb

    u_cm = []
    for t in range(n_tiles):
        by_step = [u_ref[pl.ds(t * steps_per_tile + i, nk, stride=SSM_CHUNK), :] for i in range(steps_per_tile)]
        u_cm.append(_swap_array_and_lane_block(by_step, SSM_GROUP))
    for g in range(gb):
        u_g = jnp.concatenate([u_cm[t][g] for t in range(n_tiles)], axis=1).astype(BF16)
        r = jnp.dot(u_g, w1_ref[g], preferred_element_type=F32)
        yc_ref[g] = r[:, 0:n_in]
        s_ref[pl.ds(g, nk, stride=gb), :] = r[:, n_in:n_in + n_state2]
        ss_ref[pl.ds(g, nk, stride=gb), :] = r[:, n_in + n_state2:n_in + 2 * n_state2]
    a = coef_ref[0]
    b = coef_ref[1]

    def step(k, carry):
        x, xs = carry
        r0 = pl.multiple_of(k * gb, gb)
        add = s_ref[pl.ds(r0, gb), :]
        adds = ss_ref[pl.ds(r0, gb), :]
        s_ref[pl.ds(r0, gb), :] = x
        return a * x + b * xs + add, a * xs - b * x + adds

    zero = jnp.zeros((gb, n_state2), F32)
    x_fin, _ = lax.fori_loop(0, nk, step, (zero, zero))
    hfin_ref[0] = x_fin
    for g in range(gb):
        h_prev = s_ref[pl.ds(g, nk, stride=gb), :]
        yc_ref[g] += jnp.dot(h_prev.astype(BF16), w2_ref[g], preferred_element_type=F32)
    for t in range(n_tiles):
        by_group = [yc_ref[g, :, t * LANES:(t + 1) * LANES] for g in range(gb)]
        by_step = _swap_array_and_lane_block(by_group, SSM_GROUP)
        for i in range(steps_per_tile):
            y_ref[pl.ds(t * steps_per_tile + i, nk, stride=SSM_CHUNK), :] = by_step[i]


def _ssm_seq(u, w1, w2, coef, n_seq):
    m, width = u.shape
    seq = m // n_seq
    g = w1.shape[0]
    gb = SSM_GROUP_BLOCK
    nk = seq // SSM_CHUNK
    n_in = SSM_CHUNK * SSM_GROUP
    n_state2 = 2 * SSM_STATE
    tok = pl.BlockSpec((seq, LANES), lambda i, j: (i, j))
    return pl.pallas_call(
        _ssm_seq_kernel,
        grid=(n_seq, g // gb),
        in_specs=[tok,
                  pl.BlockSpec((gb, n_in, n_in + 2 * n_state2), lambda i, j: (j, 0, 0)),
                  pl.BlockSpec((gb, n_state2, n_in), lambda i, j: (j, 0, 0)),
                  pl.BlockSpec((4, gb, n_state2), lambda i, j: (0, j, 0))],
        out_specs=[tok, pl.BlockSpec((1, gb, n_state2), lambda i, j: (i, j, 0))],
        out_shape=[jax.ShapeDtypeStruct((m, width), F32),
                   jax.ShapeDtypeStruct((n_seq, g, n_state2), F32)],
        scratch_shapes=[pltpu.VMEM((nk * gb, n_state2), F32), pltpu.VMEM((nk * gb, n_state2), F32),
                        pltpu.VMEM((gb, nk, n_in), F32)],
        compiler_params=_params(("parallel", "parallel")),
        name="ssm_seq",
    )(u, w1, w2, coef)


def _ssm_step_kernel(u_ref, h0_ref, h0s_ref, w1_ref, w2_ref, coef_ref, y_ref, hfin_ref):
    gb = u_ref.shape[0]
    n_half = u_ref.shape[2]
    n_in = 2 * n_half
    n_state2 = 2 * SSM_STATE
    for g in range(gb):
        u = u_ref[g]
        h0 = h0_ref[g]
        t_half = w1_ref[g, 0:n_half, 0:n_half]
        p_half = w1_ref[g, n_half:n_in, n_in:n_in + n_state2]
        q_half = w2_ref[g, :, 0:n_half]
        y_ref[g] = (jnp.dot(u, t_half, preferred_element_type=F32)
                    + jnp.dot(h0.astype(BF16), q_half, preferred_element_type=F32))
        hfin_ref[g] = (coef_ref[2, g:g + 1, :] * h0 + coef_ref[3, g:g + 1, :] * h0s_ref[g]
                       + jnp.dot(u, p_half, preferred_element_type=F32))


def _ssm_step(u8, h0, h0s, w1, w2, coef):
    g, n, n_half = u8.shape
    gb = SSM_GROUP_BLOCK
    n_state2 = 2 * SSM_STATE
    n_in = 2 * n_half
    blk = lambda s1, s2: pl.BlockSpec((gb, s1, s2), lambda j: (j, 0, 0))
    return pl.pallas_call(
        _ssm_step_kernel,
        grid=(g // gb,),
        in_specs=[blk(n, n_half), blk(n, n_state2), blk(n, n_state2),
                  blk(n_in, n_in + 2 * n_state2), blk(n_state2, n_in),
                  pl.BlockSpec((4, gb, n_state2), lambda j: (0, j, 0))],
        out_specs=[blk(n, n_half), blk(n, n_state2)],
        out_shape=[jax.ShapeDtypeStruct((g, n, n_half), F32), jax.ShapeDtypeStruct((g, n, n_state2), F32)],
        compiler_params=_params(("parallel",)),
        name="ssm_step",
    )(u8, h0, h0s, w1, w2, coef)


def _s5_post_kernel(y_ref, u_ref, z_ref, x_ref, d_ref, wg_ref, bg_ref, wo_ref, gp_ref, o_ref):
    y = jax.nn.gelu(y_ref[...] + d_ref[...] * u_ref[...])
    gate = jnp.dot(y.astype(BF16), wg_ref[...], preferred_element_type=F32) + bg_ref[...]
    y = y * jax.nn.sigmoid(gate)
    z = z_ref[...].astype(F32)
    y = y * (z * jax.nn.sigmoid(z))
    o = jnp.dot(y.astype(BF16), wo_ref[...], preferred_element_type=F32)
    o_ref[...] = x_ref[...] + (o * _rms_scale(o)) * gp_ref[...]


def _s5_post(y, u, z, x, d_skip, w_glu, b_glu, w_out, g_post, block_rows):
    m, d = x.shape
    tm = min(block_rows, m)
    row = pl.BlockSpec((tm, d), lambda i: (i, 0))
    vec = pl.BlockSpec((1, d), lambda i: (0, 0))
    mat = pl.BlockSpec((d, d), lambda i: (0, 0))
    return pl.pallas_call(
        _s5_post_kernel,
        grid=(m // tm,),
        in_specs=[row, row, row, row, vec, mat, vec, mat, vec],
        out_specs=row,
        out_shape=jax.ShapeDtypeStruct((m, d), F32),
        compiler_params=_params(("parallel",)),
        name="s5_post",
    )(y, u, z, x, d_skip.reshape(1, d), w_glu, b_glu.reshape(1, d), w_out, g_post.reshape(1, d))


def _sb_post_kernel(o_ref, g_ref, x_ref, wo_ref, gp_ref, out_ref):
    g = g_ref[...].astype(F32)
    y = o_ref[...].astype(F32) * (g * jax.nn.sigmoid(g))
    r = jnp.dot(y.astype(BF16), wo_ref[...], preferred_element_type=F32)
    out_ref[...] = x_ref[...] + (r * _rms_scale(r)) * gp_ref[...]


def _sb_post(o, g, x, w_out, g_post, block_rows):
    m, d = x.shape
    tm = min(block_rows, m)
    row = pl.BlockSpec((tm, d), lambda i: (i, 0))
    return pl.pallas_call(
        _sb_post_kernel,
        grid=(m // tm,),
        in_specs=[row, row, row, pl.BlockSpec((d, d), lambda i: (0, 0)), pl.BlockSpec((1, d), lambda i: (0, 0))],
        out_specs=row,
        out_shape=jax.ShapeDtypeStruct((m, d), F32),
        compiler_params=_params(("parallel",)),
        name="sb_post",
    )(o, g, x, w_out, g_post.reshape(1, d))


NEG_LOG2_E = -1.4426950408889634


def _suffix_mask(n):
    j = lax.broadcasted_iota(jnp.int32, (2 * n, n), 0)
    s = lax.broadcasted_iota(jnp.int32, (2 * n, n), 1)
    return ((j >= s) & ((j < n) | (j >= s + n))).astype(BF16)


def _miss(z, mask):
    sp = jnp.maximum(z, 0.0) + jnp.log(1.0 + jnp.exp2(jnp.abs(z) * NEG_LOG2_E))
    return sp if mask is None else jnp.where(mask, sp, 0.0)


def _suffix_sums(miss, tri):
    hi = miss.astype(BF16)
    lo = (miss - hi.astype(F32)).astype(BF16)
    return jnp.dot(jnp.concatenate([hi, lo], axis=1), tri, preferred_element_type=F32)


def _stick_weights(z, misses_from_here, mask):
    w = jnp.exp(z - misses_from_here)
    return w if mask is None else jnp.where(mask, w, 0.0)


def _attn_prompt_kernel(bias_ref, q_ref, kt_ref, vt_ref, o_ref, acc_ref, later_ref):
    tb = ATT_BLOCK
    hp = pl.program_id(1)
    qi = pl.program_id(2)
    q = q_ref[0]
    lane = lax.broadcasted_iota(jnp.int32, (1, LANES), 1)
    tri = _suffix_mask(tb)
    diag = lax.broadcasted_iota(jnp.int32, (tb, tb), 1) < lax.broadcasted_iota(jnp.int32, (tb, tb), 0)
    heads = range(HEADS_PER_LANE_TILE)
    q_aug = []
    for hh in heads:
        qm = jnp.where((lane >= hh * HEAD_DIM) & (lane < (hh + 1) * HEAD_DIM), q, jnp.zeros_like(q))
        rest = jnp.full((1, LANES), bias_ref[hp * HEADS_PER_LANE_TILE + hh], F32)
        terms = jnp.zeros((1, LANES), F32)
        for t in range(BIAS_TERMS):
            term = rest.astype(BF16).astype(F32)
            terms = jnp.where(lane == t, term, terms)
            rest = rest - term
        q_aug.append(jnp.concatenate([qm, jnp.broadcast_to(terms, qm.shape).astype(BF16)], axis=1))
    ones_rows = (lax.broadcasted_iota(jnp.int32, (LANES, tb), 0) < BIAS_TERMS).astype(BF16)
    neg_tri = -tri
    acc_ref[...] = jnp.zeros_like(acc_ref)
    later_ref[...] = jnp.zeros_like(later_ref)

    def visit(blocks):
        chains = []
        for kj, subs_masks in blocks:
            kb = jnp.concatenate([kt_ref[0, 0, kj], ones_rows], axis=0)
            kb_minus_tri = jnp.concatenate([kb, neg_tri], axis=0)
            vb = vt_ref[0, 0, kj]
            chains += [(hh, slice(sub * tb, (sub + 1) * tb), mask, kb, kb_minus_tri, vb)
                       for hh in heads for sub, mask in subs_masks]
        zs = [jnp.dot(q_aug[hh][rows], kb, preferred_element_type=F32) for hh, rows, _, kb, _, _ in chains]
        hi_lo, z0 = [], []
        for z, chain in zip(zs, chains):
            miss = _miss(z, chain[2])
            hi = miss.astype(BF16)
            hi_lo.append(jnp.concatenate([hi, (miss - hi.astype(F32)).astype(BF16)], axis=1))
            z0.append(z[:, 0:1])
        log_ws = [jnp.dot(jnp.concatenate([q_aug[hh][rows], hl], axis=1), kmt, preferred_element_type=F32)
                  for hl, (hh, rows, _, _, kmt, _) in zip(hi_lo, chains)]
        for lw, z_first, (hh, rows, mask, _, _, vb) in zip(log_ws, z0, chains):
            c = later_ref[hh, rows]
            w = jnp.exp(lw - c)
            w = (w if mask is None else jnp.where(mask, w, 0.0)).astype(BF16)
            acc_ref[hh, rows] += lax.dot_general(w, vb, (((1,), (1,)), ((), ())), preferred_element_type=F32)
            later_ref[hh, rows] = c + (z_first - lw[:, 0:1])

    first = qi * Q_SUBS
    visit([(first + lead, [(lead, diag)] + [(sub, None) for sub in range(lead + 1, Q_SUBS)])
           for lead in reversed(range(Q_SUBS))])

    all_subs = [(sub, None) for sub in range(Q_SUBS)]

    def body(t, carry):
        kj = first - 1 - LOOP_BLOCKS * t
        visit([(kj - i, all_subs) for i in range(LOOP_BLOCKS)])
        return carry

    lax.fori_loop(0, first // LOOP_BLOCKS, body, 0)
    o_ref[0] = jnp.where(lane < HEAD_DIM, acc_ref[0], acc_ref[1]).astype(o_ref.dtype)


def _attn_prompt(q, kt, vt, bias):
    b, s, hd = q.shape
    n_kb = kt.shape[2]
    tq = Q_SUBS * ATT_BLOCK
    qspec = pl.BlockSpec((1, tq, LANES), lambda i, h, j: (i, j, h))
    kspec = pl.BlockSpec((1, 1, n_kb, LANES, ATT_BLOCK), lambda i, h, j: (i, h, 0, 0, 0))
    return pl.pallas_call(
        _attn_prompt_kernel,
        grid=(b, hd // LANES, s // tq),
        in_specs=[pl.BlockSpec(memory_space=pltpu.SMEM), qspec, kspec, kspec],
        out_specs=qspec,
        out_shape=jax.ShapeDtypeStruct((b, s, hd), BF16),
        scratch_shapes=[pltpu.VMEM((HEADS_PER_LANE_TILE, tq, LANES), F32),
                        pltpu.VMEM((HEADS_PER_LANE_TILE, tq, 1), F32)],
        compiler_params=_params(("parallel", "parallel", "arbitrary")),
        name="attn_prompt",
    )(bias, q, kt, vt)


def _attn_paged_kernel(pt_ref, q_ref, knew_ref, vnew_ref, *rest, pages_per_step):
    del pt_ref
    kpage_refs = rest[:pages_per_step]
    vpage_refs = rest[pages_per_step:2 * pages_per_step]
    bias_ref, o_ref, qbd_ref, acc_ref, later_ref = rest[2 * pages_per_step:]
    j = pl.program_id(1)
    nq = q_ref.shape[0]
    hd = q_ref.shape[1]
    n_heads = hd // HEAD_DIM
    rows = n_heads * nq
    tk = knew_ref.shape[1]
    row = lax.broadcasted_iota(jnp.int32, (rows, 1), 0)
    tri = _suffix_mask(tk)
    contract_last = (((1,), (1,)), ((), ()))

    @pl.when(j == 0)
    def _():
        lane = lax.broadcasted_iota(jnp.int32, (1, hd), 1)
        q_rep = jnp.concatenate([q_ref[...]] * n_heads, axis=0)
        qbd = jnp.where(lane // HEAD_DIM == row // nq, q_rep, 0.0).astype(BF16)
        qbd_ref[...] = qbd
        col = lax.broadcasted_iota(jnp.int32, (1, tk), 1)
        mask = col < row % nq
        z = lax.dot_general(qbd, knew_ref[0].astype(BF16), contract_last, preferred_element_type=F32) + bias_ref[...]
        sums = _suffix_sums(_miss(z, mask), tri)
        w = _stick_weights(z, sums, mask).astype(BF16)
        acc_ref[...] = jnp.dot(w, vnew_ref[0].astype(BF16), preferred_element_type=F32)
        later_ref[...] = sums[:, 0:1]

    @pl.when(j > 0)
    def _():
        order = list(reversed(range(pages_per_step)))
        zs = [jnp.dot(qbd_ref[...], kpage_refs[c][0].reshape(hd, tk).astype(BF16), preferred_element_type=F32)
              + bias_ref[...] for c in order]
        sums = [_suffix_sums(_miss(z, None), tri) for z in zs]
        later_blocks = later_ref[...]
        ws = []
        for z, sm in zip(zs, sums):
            ws.append(_stick_weights(z, sm + later_blocks, None).astype(BF16))
            later_blocks = later_blocks + sm[:, 0:1]
        later_ref[...] = later_blocks
        v_all = jnp.concatenate([vpage_refs[c][0].reshape(hd, tk).astype(BF16) for c in order], axis=1)
        acc_ref[...] += lax.dot_general(jnp.concatenate(ws, axis=1), v_all, contract_last,
                                        preferred_element_type=F32)

    @pl.when(j == pl.num_programs(1) - 1)
    def _():
        lane = lax.broadcasted_iota(jnp.int32, (1, hd), 1)
        out = jnp.zeros((nq, hd), F32)
        for h in range(n_heads):
            out = out + jnp.where(lane // HEAD_DIM == h, acc_ref[h * nq:(h + 1) * nq, :], 0.0)
        o_ref[...] = out


def _attn_paged(q, k_new, v_new, cache_kt, cache_vt, page_table, bias, n_new):
    n_seq, n_pages = page_table.shape
    hd = q.shape[1]
    n_heads = hd // HEAD_DIM
    rows = n_heads * n_new
    page = cache_kt.shape[3]
    pps = PAGES_PER_STEP
    seq_rows = pl.BlockSpec((n_new, hd), lambda n, j, pt: (n, 0))
    new_spec = pl.BlockSpec((1, page, hd), lambda n, j, pt: (n, 0, 0))

    def page_spec(c):
        return pl.BlockSpec((1, n_heads, HEAD_DIM, page),
                            lambda n, j, pt: (pt[n, n_pages - jnp.maximum(j, 1) * pps + c], 0, 0, 0))

    grid_spec = pltpu.PrefetchScalarGridSpec(
        num_scalar_prefetch=1,
        grid=(n_seq, n_pages // pps + 1),
        in_specs=([seq_rows, new_spec, new_spec] + [page_spec(c) for c in range(pps)] * 2
                  + [pl.BlockSpec((rows, 1), lambda n, j, pt: (0, 0))]),
        out_specs=seq_rows,
        scratch_shapes=[pltpu.VMEM((rows, hd), BF16), pltpu.VMEM((rows, hd), F32), pltpu.VMEM((rows, 1), F32)],
    )
    return pl.pallas_call(
        functools.partial(_attn_paged_kernel, pages_per_step=pps),
        grid_spec=grid_spec,
        out_shape=jax.ShapeDtypeStruct((n_seq * n_new, hd), F32),
        compiler_params=_params(("parallel", "arbitrary")),
        name="attn_paged",
    )(page_table, q, k_new, v_new, *([cache_kt] * pps), *([cache_vt] * pps),
      jnp.repeat(bias, n_new).reshape(rows, 1))


def _s5_layer(xp, xs, n_prompt, h0_re, h0_im, prm):
    (g_pre, g_post, w_in, log_dt, a_re, a_im, b_re, b_im, c_re, c_im, d_skip, w_glu, b_glu, w_out) = prm
    d = xp.shape[1]
    n_groups, n_state = a_re.shape
    gc = b_re.shape[-1]
    width = n_groups * gc
    w1, w2, coef = _ssm_prep(log_dt, a_re, a_im, b_re, b_im, c_re, c_im)
    w_in_b, w_glu_b, w_out_b = w_in.astype(BF16), w_glu.astype(BF16), w_out.astype(BF16)
    uz_outs = [(0, width, 1.0, F32), (width, 2 * width, 1.0, BF16)]

    u, z = _norm_matmul(xp, g_pre, w_in_b, uz_outs, 512)
    y, hfin = _ssm_seq(u, w1, w2, coef, n_prompt)
    xp = _s5_post(y, u, z, xp, d_skip, w_glu_b, b_glu, w_out_b, g_post, 512)

    n_seq = h0_re.shape[0]
    t_new = xs.shape[0] // n_seq
    us, zs = _norm_matmul(xs, g_pre, w_in_b, uz_outs, 512)
    u8 = (us.reshape(n_seq, t_new, n_groups, gc).transpose(2, 0, 1, 3)
          .reshape(n_groups, n_seq, t_new * gc).astype(BF16))
    h0 = jnp.concatenate([h0_re, h0_im], axis=-1).transpose(1, 0, 2)
    h0s = jnp.concatenate([h0_im, h0_re], axis=-1).transpose(1, 0, 2)
    y8, hf8 = _ssm_step(u8, h0, h0s, w1, w2, coef)
    ys = y8.reshape(n_groups, n_seq, t_new, gc).transpose(1, 2, 0, 3).reshape(xs.shape[0], width)
    xs = _s5_post(ys, us, zs, xs, d_skip, w_glu_b, b_glu, w_out_b, g_post, 512)
    hf8 = hf8.transpose(1, 0, 2)
    return (xp, xs, hfin[..., :n_state], hfin[..., n_state:], hf8[..., :n_state], hf8[..., n_state:])


def kernel(x_prompt, x_sample, state_ssm_re, state_ssm_im, cache_k, cache_v, page_table, a_norm_pre, a_norm_post, a_w_in, a_log_dt, a_A_re, a_A_im, a_B_re, a_B_im, a_C_re, a_C_im, a_D, a_w_glu, a_b_glu, a_w_out, kv_norm, w_kv, b_norm_pre, b_norm_post, b_w_in, b_logit_bias, b_w_out):
    n_prompt, seq, d = x_prompt.shape
    n_seq, t_new, _ = x_sample.shape
    n_a = a_w_in.shape[0]
    n_b = b_w_in.shape[0]
    att = w_kv.shape[1] // 2
    n_heads = att // HEAD_DIM
    assert t_new * 2 == SSM_CHUNK and cache_k.shape[1] == PAGE_SIZE and HEAD_DIM ** -0.5 == 0.125

    xp = x_prompt.reshape(n_prompt * seq, d)
    xs = x_sample.reshape(n_seq * t_new, d)
    p_re, p_im, s_re, s_im = [], [], [], []
    for i in range(n_a):
        prm = (a_norm_pre[i], a_norm_post[i], a_w_in[i], a_log_dt[i], a_A_re[i], a_A_im[i], a_B_re[i], a_B_im[i],
               a_C_re[i], a_C_im[i], a_D[i], a_w_glu[i], a_b_glu[i], a_w_out[i])
        xp, xs, hr, hi, sr, si = _s5_layer(xp, xs, n_prompt, state_ssm_re[i], state_ssm_im[i], prm)
        p_re.append(hr)
        p_im.append(hi)
        s_re.append(sr)
        s_im.append(si)

    w_kv_b = w_kv.astype(BF16)
    kt_p, vt_p, kt_pb, vt_pb = _kv_proj(xp, kv_norm, w_kv_b.T, n_prompt, 512)
    k_p = kt_p.reshape(n_prompt, n_heads, HEAD_DIM, seq).transpose(0, 3, 1, 2)
    v_p = vt_p.reshape(n_prompt, n_heads, HEAD_DIM, seq).transpose(0, 3, 1, 2)
    k_s, v_s = _norm_matmul(xs, kv_norm, w_kv_b, [(0, att, 1.0, F32), (att, 2 * att, 1.0, F32)], 512)
    pad_new = lambda a: jnp.pad(a.reshape(n_seq, t_new, att), ((0, 0), (0, PAGE_SIZE - t_new), (0, 0)))
    k_s_pad, v_s_pad = pad_new(k_s), pad_new(v_s)
    cache_kt = cache_k.transpose(0, 2, 3, 1)
    cache_vt = cache_v.transpose(0, 2, 3, 1)

    scale = HEAD_DIM ** -0.5
    for j in range(n_b):
        w_in_b, w_out_b = b_w_in[j].astype(BF16), b_w_out[j].astype(BF16)
        q, g = _norm_matmul(xp, b_norm_pre[j], w_in_b, [(0, att, scale, BF16), (att, 2 * att, 1.0, BF16)], 512)
        o = _attn_prompt(q.reshape(n_prompt, seq, att), kt_pb, vt_pb, b_logit_bias[j])
        xp = _sb_post(o.reshape(n_prompt * seq, att), g, xp, w_out_b, b_norm_post[j], 512)
        qs, gs = _norm_matmul(xs, b_norm_pre[j], w_in_b, [(0, att, scale, F32), (att, 2 * att, 1.0, F32)], 512)
        os_ = _attn_paged(qs, k_s_pad, v_s_pad, cache_kt, cache_vt, page_table, b_logit_bias[j], t_new)
        xs = _sb_post(os_, gs, xs, w_out_b, b_norm_post[j], 512)

    return (xp.reshape(n_prompt, seq, d), xs.reshape(n_seq, t_new, d),
            jnp.stack(p_re), jnp.stack(p_im),
            k_p, v_p,
            jnp.stack(s_re), jnp.stack(s_im),
            k_s.reshape(n_seq, t_new, n_heads, HEAD_DIM), v_s.reshape(n_seq, t_new, n_heads, HEAD_DIM))
```

```python
import functools

import jax
import jax.numpy as jnp
from jax import lax
from jax.experimental import pallas as pl
from jax.experimental.pallas import tpu as pltpu

F32 = jnp.float32
BF16 = jnp.bfloat16

RMS_EPS = 1e-6
SSM_GROUP = 16
SSM_STATE = 64
SSM_CHUNK = 16
SSM_GROUP_BLOCK = 8
LAM_POWERS = 5
TAB_ROWS = 16
HEAD_DIM = 64
LANES = 128
HEADS_PER_LANE_TILE = LANES // HEAD_DIM
ATT_BLOCK = 256
Q_SUBS = 4
LOOP_BLOCKS = 4
BIAS_TERMS = 3
PAGE_SIZE = 128
PAGES_PER_STEP = 8
PAGED_HEAD_GROUPS = 4
VMEM_LIMIT = 48 * 1024 * 1024


def _params(semantics):
    return pltpu.CompilerParams(dimension_semantics=semantics, vmem_limit_bytes=VMEM_LIMIT)


def _rms_scale(x):
    return lax.rsqrt(jnp.mean(x * x, axis=-1, keepdims=True) + RMS_EPS)


def _norm_matmul_kernel(x_ref, g_ref, w_ref, *o_refs, splits):
    x = x_ref[...]
    h = (x * _rms_scale(x)) * g_ref[...]
    acc = jnp.dot(h.astype(BF16), w_ref[...], preferred_element_type=F32)
    for o_ref, (lo, hi, scale) in zip(o_refs, splits):
        v = acc[:, lo:hi]
        if scale != 1.0:
            v = v * scale
        o_ref[...] = v.astype(o_ref.dtype)


def _norm_matmul(x, g, w, outs, block_rows):
    m, d = x.shape
    n = w.shape[1]
    tm = min(block_rows, m)
    splits = tuple((lo, hi, scale) for lo, hi, scale, _ in outs)
    return pl.pallas_call(
        functools.partial(_norm_matmul_kernel, splits=splits),
        grid=(m // tm,),
        in_specs=[pl.BlockSpec((tm, d), lambda i: (i, 0)),
                  pl.BlockSpec((1, d), lambda i: (0, 0)),
                  pl.BlockSpec((d, n), lambda i: (0, 0))],
        out_specs=[pl.BlockSpec((tm, hi - lo), lambda i: (i, 0)) for lo, hi, _, _ in outs],
        out_shape=[jax.ShapeDtypeStruct((m, hi - lo), dt) for lo, hi, _, dt in outs],
        compiler_params=_params(("parallel",)),
        name="norm_matmul",
    )(x, g.reshape(1, d), w)


def _kv_proj_kernel(x_ref, g_ref, wt_ref, kt_ref, vt_ref, ktb_ref, vtb_ref):
    x = x_ref[...]
    h = (x * _rms_scale(x)) * g_ref[...]
    acc = lax.dot_general(wt_ref[...], h.astype(BF16), (((1,), (1,)), ((), ())), preferred_element_type=F32)
    att = kt_ref.shape[1]
    n_blocks = ktb_ref.shape[2]
    for t_ref, b_ref, lo in ((kt_ref, ktb_ref, 0), (vt_ref, vtb_ref, att)):
        part = acc[lo:lo + att]
        t_ref[0] = part
        for c in range(n_blocks):
            blk = part[:, c * ATT_BLOCK:(c + 1) * ATT_BLOCK]
            b_ref[0, :, c] = blk.reshape(att // LANES, LANES, ATT_BLOCK).astype(b_ref.dtype)


def _kv_proj(x, g, wt, n_prompt, block_rows):
    m, d = x.shape
    att = wt.shape[0] // 2
    seq = m // n_prompt
    tm = min(block_rows, seq)
    steps = seq // tm
    n_blocks = tm // ATT_BLOCK
    t_spec = pl.BlockSpec((1, att, tm), lambda b, i: (b, 0, i))
    b_spec = pl.BlockSpec((1, att // LANES, n_blocks, LANES, ATT_BLOCK), lambda b, i: (b, 0, i, 0, 0))
    blocked = jax.ShapeDtypeStruct((n_prompt, att // LANES, seq // ATT_BLOCK, LANES, ATT_BLOCK), BF16)
    return pl.pallas_call(
        _kv_proj_kernel,
        grid=(n_prompt, steps),
        in_specs=[pl.BlockSpec((tm, d), lambda b, i: (b * steps + i, 0)),
                  pl.BlockSpec((1, d), lambda b, i: (0, 0)),
                  pl.BlockSpec((2 * att, d), lambda b, i: (0, 0))],
        out_specs=[t_spec, t_spec, b_spec, b_spec],
        out_shape=[jax.ShapeDtypeStruct((n_prompt, att, seq), F32)] * 2 + [blocked] * 2,
        compiler_params=_params(("parallel", "parallel")),
        name="kv_proj",
    )(x, g.reshape(1, d), wt)


def _ssm_lam_kernel(log_dt_ref, a_re_ref, a_im_ref, tab_ref):
    a_re = a_re_ref[...]
    a_im = a_im_ref[...]
    dt = jnp.exp(log_dt_ref[...])
    mag = jnp.exp(dt * a_re)
    lr = mag * jnp.cos(dt * a_im)
    li = mag * jnp.sin(dt * a_im)
    den = a_re * a_re + a_im * a_im
    tab_ref[0] = ((lr - 1.0) * a_re + li * a_im) / den
    tab_ref[1] = (li * a_re - (lr - 1.0) * a_im) / den
    for b in range(LAM_POWERS):
        tab_ref[2 + 2 * b] = lr
        tab_ref[3 + 2 * b] = li
        lr, li = lr * lr - li * li, 2.0 * (lr * li)


def _lam_pow(tau, pow_re, pow_im, shape):
    pr = pi = None
    for b in range(LAM_POWERS - 1):
        on = ((tau >> b) & 1) == 1
        fr = jnp.where(on, jnp.broadcast_to(pow_re[b], shape), 1.0)
        fi = jnp.where(on, jnp.broadcast_to(pow_im[b], shape), 0.0)
        pr, pi = (fr, fi) if pr is None else (pr * fr - pi * fi, pr * fi + pi * fr)
    return pr, pi


def _ssm_prep_kernel(colp_ref, rowp_ref, ct_re_ref, ct_im_ref, bt_re_ref, bt_im_ref,
                     w1_ref, w2_ref, coef_ref):
    n_state2 = 2 * SSM_STATE
    n_in = SSM_CHUNK * SSM_GROUP
    col = lambda i: colp_ref[0, :, i:i + 1]
    tau_l = lax.broadcasted_iota(jnp.int32, (1, n_in), 1) // SSM_GROUP
    row_is_re = lax.broadcasted_iota(jnp.int32, (n_state2, 1), 0) < SSM_STATE
    rep_lanes = (lax.broadcasted_iota(jnp.int32, (SSM_GROUP, n_in), 1) % SSM_GROUP
                 == lax.broadcasted_iota(jnp.int32, (SSM_GROUP, n_in), 0)).astype(F32)
    spread = lambda a: jnp.dot(jnp.concatenate([a, a], axis=0), rep_lanes, preferred_element_type=F32,
                               precision=lax.Precision.HIGHEST)
    ct_re = spread(ct_re_ref[0])
    ct_im = spread(ct_im_ref[0])
    pr, pi = _lam_pow(tau_l, [col(2 + 2 * b) for b in range(LAM_POWERS)],
                      [col(3 + 2 * b) for b in range(LAM_POWERS)], (n_state2, n_in))
    g_re, g_im = ct_re * pr - ct_im * pi, ct_re * pi + ct_im * pr
    rhs_k = jnp.where(row_is_re, g_re, g_im)
    q_re, q_im = g_re * col(2) - g_im * col(3), g_re * col(3) + g_im * col(2)
    w2_ref[0] = jnp.where(row_is_re, q_re, -q_im).astype(w2_ref.dtype)

    row = lambda i: rowp_ref[0, i:i + 1, :]
    lane_is_re = lax.broadcasted_iota(jnp.int32, (1, n_state2), 1) < SSM_STATE
    w_re, w_im = row(0), row(1)
    rep_halves = (lax.broadcasted_iota(jnp.int32, (SSM_STATE, n_state2), 1) % SSM_STATE
                  == lax.broadcasted_iota(jnp.int32, (SSM_STATE, n_state2), 0)).astype(F32)
    stack = lambda a: jnp.concatenate([jnp.dot(a, rep_halves, preferred_element_type=F32,
                                               precision=lax.Precision.HIGHEST)] * SSM_CHUNK, axis=0)
    bt_re = stack(bt_re_ref[0])
    bt_im = stack(bt_im_ref[0])
    bb_re = w_re * bt_re - w_im * bt_im
    bb_im = w_re * bt_im + w_im * bt_re

    lhs_k = jnp.where(lane_is_re, bb_re[0:SSM_GROUP], -bb_im[0:SSM_GROUP])
    kk = jnp.dot(lhs_k, rhs_k, preferred_element_type=F32, precision=lax.Precision.HIGHEST)
    lane_in = lax.broadcasted_iota(jnp.int32, (1, n_in), 1)
    for s in range(SSM_CHUNK):
        shifted = kk if s == 0 else pltpu.roll(kk, s * SSM_GROUP, 1)
        piece = jnp.where(lane_in >= s * SSM_GROUP, shifted, 0.0)
        w1_ref[0, s * SSM_GROUP:(s + 1) * SSM_GROUP, 0:n_in] = piece.astype(w1_ref.dtype)

    tau_row = SSM_CHUNK - 1 - lax.broadcasted_iota(jnp.int32, (n_in, 1), 0) // SSM_GROUP
    pr_r, pi_r = _lam_pow(tau_row, [row(2 + 2 * b) for b in range(LAM_POWERS)],
                          [row(3 + 2 * b) for b in range(LAM_POWERS)], (n_in, n_state2))
    p_re = bb_re * pr_r - bb_im * pi_r
    p_im = bb_re * pi_r + bb_im * pr_r
    w1_ref[0, :, n_in:n_in + n_state2] = jnp.where(lane_is_re, p_re, p_im).astype(w1_ref.dtype)
    w1_ref[0, :, n_in + n_state2:n_in + 2 * n_state2] = jnp.where(lane_is_re, p_im, p_re).astype(w1_ref.dtype)

    full, half = 2 * (LAM_POWERS - 1), 2 * (LAM_POWERS - 2)
    zero = jnp.zeros_like(w_re)
    coef_ref[0] = jnp.concatenate(
        [row(2 + full), jnp.where(lane_is_re, -row(3 + full), row(3 + full)),
         row(2 + half), jnp.where(lane_is_re, -row(3 + half), row(3 + half)), zero, zero, zero, zero], axis=0)


def _ssm_prep(log_dt, a_re, a_im, b_re, b_im, c_re, c_im):
    g, p = a_re.shape
    c = b_re.shape[-1]
    n_in = SSM_CHUNK * c
    n_tab = 2 + 2 * LAM_POWERS
    tab = pl.pallas_call(
        _ssm_lam_kernel,
        out_shape=jax.ShapeDtypeStruct((n_tab, g, p), F32),
        name="ssm_lam",
    )(log_dt.reshape(g, 1), a_re, a_im)
    tab2 = jnp.concatenate([tab, tab], axis=-1)
    pad = TAB_ROWS - n_tab
    cols = jnp.pad(tab2.transpose(1, 2, 0), ((0, 0), (0, 0), (0, pad)))
    rows = jnp.pad(tab2.transpose(1, 0, 2), ((0, 0), (0, pad), (0, 0)))
    ct = lambda a: jnp.swapaxes(a, 1, 2)
    bt = lambda a: jnp.swapaxes(a, 1, 2)
    spec3 = lambda s1, s2: pl.BlockSpec((1, s1, s2), lambda i: (i, 0, 0))
    w1, w2, coef = pl.pallas_call(
        _ssm_prep_kernel,
        grid=(g,),
        in_specs=[spec3(2 * p, TAB_ROWS), spec3(TAB_ROWS, 2 * p), spec3(p, c), spec3(p, c), spec3(c, p), spec3(c, p)],
        out_specs=[spec3(n_in, n_in + 4 * p), spec3(2 * p, n_in), spec3(8, 2 * p)],
        out_shape=[jax.ShapeDtypeStruct((g, n_in, n_in + 4 * p), BF16),
                   jax.ShapeDtypeStruct((g, 2 * p, n_in), BF16),
                   jax.ShapeDtypeStruct((g, 8, 2 * p), F32)],
        compiler_params=_params(("parallel",)),
        name="ssm_prep",
    )(cols, rows, ct(c_re), ct(c_im), bt(b_re), bt(b_im))
    return w1, w2, jnp.swapaxes(coef, 0, 1)[:4]


def _swap_array_and_lane_block(xs, lane_block_width):
    n = len(xs)
    lanes = n * lane_block_width
    blk = lax.broadcasted_iota(jnp.int32, (1, lanes), 1) // lane_block_width
    d = n // 2
    while d >= 1:
        keep = (blk & d) == 0
        new = list(xs)
        for i in range(n):
            if i & d == 0:
                lo, hi = xs[i], xs[i + d]
                if 2 * d == n:
                    moved = pltpu.roll(jnp.where(keep, hi, lo), d * lane_block_width, 1)
                    new[i] = jnp.where(keep, lo, moved)
                    new[i + d] = jnp.where(keep, moved, hi)
                else:
                    new[i] = jnp.where(keep, lo, pltpu.roll(hi, d * lane_block_width, 1))
                    new[i + d] = jnp.where(keep, pltpu.roll(lo, lanes - d * lane_block_width, 1), hi)
        xs = new
        d //= 2
    return xs


def _ssm_seq_kernel(u_ref, w1_ref, w2_ref, coef_ref, y_ref, hfin_ref, s_ref, ss_ref, yc_ref):
    gb = SSM_GROUP_BLOCK
    nk = u_ref.shape[0] // SSM_CHUNK
    n_in = SSM_CHUNK * SSM_GROUP
    n_state2 = 2 * SSM_STATE
    steps_per_tile = LANES // SSM_GROUP
    n_tiles = SSM_CHUNK // steps_per_tile
    assert gb * SSM_GROUP == LANES and steps_per_tile == gb

    u_cm = []
    for t in range(n_tiles):
        by_step = [u_ref[pl.ds(t * steps_per_tile + i, nk, stride=SSM_CHUNK), :] for i in range(steps_per_tile)]
        u_cm.append(_swap_array_and_lane_block(by_step, SSM_GROUP))
    for g in range(gb):
        u_g = jnp.concatenate([u_cm[t][g] for t in range(n_tiles)], axis=1).astype(BF16)
        r = jnp.dot(u_g, w1_ref[g], preferred_element_type=F32)
        yc_ref[g] = r[:, 0:n_in]
        s_ref[pl.ds(g, nk, stride=gb), :] = r[:, n_in:n_in + n_state2]
        ss_ref[pl.ds(g, nk, stride=gb), :] = r[:, n_in + n_state2:n_in + 2 * n_state2]
    a = coef_ref[0]
    b = coef_ref[1]

    def step(k, carry):
        x, xs = carry
        r0 = pl.multiple_of(k * gb, gb)
        add = s_ref[pl.ds(r0, gb), :]
        adds = ss_ref[pl.ds(r0, gb), :]
        s_ref[pl.ds(r0, gb), :] = x
        return a * x + b * xs + add, a * xs - b * x + adds

    zero = jnp.zeros((gb, n_state2), F32)
    x_fin, _ = lax.fori_loop(0, nk, step, (zero, zero))
    hfin_ref[0] = x_fin
    for g in range(gb):
        h_prev = s_ref[pl.ds(g, nk, stride=gb), :]
        yc_ref[g] += jnp.dot(h_prev.astype(BF16), w2_ref[g], preferred_element_type=F32)
    for t in range(n_tiles):
        by_group = [yc_ref[g, :, t * LANES:(t + 1) * LANES] for g in range(gb)]
        by_step = _swap_array_and_lane_block(by_group, SSM_GROUP)
        for i in range(steps_per_tile):
            y_ref[pl.ds(t * steps_per_tile + i, nk, stride=SSM_CHUNK), :] = by_step[i]


def _ssm_seq(u, w1, w2, coef, n_seq):
    m, width = u.shape
    seq = m // n_seq
    g = w1.shape[0]
    gb = SSM_GROUP_BLOCK
    nk = seq // SSM_CHUNK
    n_in = SSM_CHUNK * SSM_GROUP
    n_state2 = 2 * SSM_STATE
    tok = pl.BlockSpec((seq, LANES), lambda i, j: (i, j))
    return pl.pallas_call(
        _ssm_seq_kernel,
        grid=(n_seq, g // gb),
        in_specs=[tok,
                  pl.BlockSpec((gb, n_in, n_in + 2 * n_state2), lambda i, j: (j, 0, 0)),
                  pl.BlockSpec((gb, n_state2, n_in), lambda i, j: (j, 0, 0)),
                  pl.BlockSpec((4, gb, n_state2), lambda i, j: (0, j, 0))],
        out_specs=[tok, pl.BlockSpec((1, gb, n_state2), lambda i, j: (i, j, 0))],
        out_shape=[jax.ShapeDtypeStruct((m, width), F32),
                   jax.ShapeDtypeStruct((n_seq, g, n_state2), F32)],
        scratch_shapes=[pltpu.VMEM((nk * gb, n_state2), F32), pltpu.VMEM((nk * gb, n_state2), F32),
                        pltpu.VMEM((gb, nk, n_in), F32)],
        compiler_params=_params(("parallel", "parallel")),
        name="ssm_seq",
    )(u, w1, w2, coef)


def _ssm_step_kernel(u_ref, h0_ref, h0s_ref, w1_ref, w2_ref, coef_ref, y_ref, hfin_ref):
    gb = u_ref.shape[0]
    n_half = u_ref.shape[2]
    n_in = 2 * n_half
    n_state2 = 2 * SSM_STATE
    for g in range(gb):
        u = u_ref[g]
        h0 = h0_ref[g]
        t_half = w1_ref[g, 0:n_half, 0:n_half]
        p_half = w1_ref[g, n_half:n_in, n_in:n_in + n_state2]
        q_half = w2_ref[g, :, 0:n_half]
        y_ref[g] = (jnp.dot(u, t_half, preferred_element_type=F32)
                    + jnp.dot(h0.astype(BF16), q_half, preferred_element_type=F32))
        hfin_ref[g] = (coef_ref[2, g:g + 1, :] * h0 + coef_ref[3, g:g + 1, :] * h0s_ref[g]
                       + jnp.dot(u, p_half, preferred_element_type=F32))


def _ssm_step(u8, h0, h0s, w1, w2, coef):
    g, n, n_half = u8.shape
    gb = SSM_GROUP_BLOCK
    n_state2 = 2 * SSM_STATE
    n_in = 2 * n_half
    blk = lambda s1, s2: pl.BlockSpec((gb, s1, s2), lambda j: (j, 0, 0))
    return pl.pallas_call(
        _ssm_step_kernel,
        grid=(g // gb,),
        in_specs=[blk(n, n_half), blk(n, n_state2), blk(n, n_state2),
                  blk(n_in, n_in + 2 * n_state2), blk(n_state2, n_in),
                  pl.BlockSpec((4, gb, n_state2), lambda j: (0, j, 0))],
        out_specs=[blk(n, n_half), blk(n, n_state2)],
        out_shape=[jax.ShapeDtypeStruct((g, n, n_half), F32), jax.ShapeDtypeStruct((g, n, n_state2), F32)],
        compiler_params=_params(("parallel",)),
        name="ssm_step",
    )(u8, h0, h0s, w1, w2, coef)


def _s5_post_kernel(y_ref, u_ref, z_ref, x_ref, d_ref, wg_ref, bg_ref, wo_ref, gp_ref, o_ref):
    y = jax.nn.gelu(y_ref[...] + d_ref[...] * u_ref[...])
    gate = jnp.dot(y.astype(BF16), wg_ref[...], preferred_element_type=F32) + bg_ref[...]
    y = y * jax.nn.sigmoid(gate)
    z = z_ref[...].astype(F32)
    y = y * (z * jax.nn.sigmoid(z))
    o = jnp.dot(y.astype(BF16), wo_ref[...], preferred_element_type=F32)
    o_ref[...] = x_ref[...] + (o * _rms_scale(o)) * gp_ref[...]


def _s5_post(y, u, z, x, d_skip, w_glu, b_glu, w_out, g_post, block_rows):
    m, d = x.shape
    tm = min(block_rows, m)
    row = pl.BlockSpec((tm, d), lambda i: (i, 0))
    vec = pl.BlockSpec((1, d), lambda i: (0, 0))
    mat = pl.BlockSpec((d, d), lambda i: (0, 0))
    return pl.pallas_call(
        _s5_post_kernel,
        grid=(m // tm,),
        in_specs=[row, row, row, row, vec, mat, vec, mat, vec],
        out_specs=row,
        out_shape=jax.ShapeDtypeStruct((m, d), F32),
        compiler_params=_params(("parallel",)),
        name="s5_post",
    )(y, u, z, x, d_skip.reshape(1, d), w_glu, b_glu.reshape(1, d), w_out, g_post.reshape(1, d))


def _sb_post_kernel(o_ref, g_ref, x_ref, wo_ref, gp_ref, out_ref):
    g = g_ref[...].astype(F32)
    y = o_ref[...].astype(F32) * (g * jax.nn.sigmoid(g))
    r = jnp.dot(y.astype(BF16), wo_ref[...], preferred_element_type=F32)
    out_ref[...] = x_ref[...] + (r * _rms_scale(r)) * gp_ref[...]


def _sb_post(o, g, x, w_out, g_post, block_rows):
    m, d = x.shape
    tm = min(block_rows, m)
    row = pl.BlockSpec((tm, d), lambda i: (i, 0))
    return pl.pallas_call(
        _sb_post_kernel,
        grid=(m // tm,),
        in_specs=[row, row, row, pl.BlockSpec((d, d), lambda i: (0, 0)), pl.BlockSpec((1, d), lambda i: (0, 0))],
        out_specs=row,
        out_shape=jax.ShapeDtypeStruct((m, d), F32),
        compiler_params=_params(("parallel",)),
        name="sb_post",
    )(o, g, x, w_out, g_post.reshape(1, d))


NEG_LOG2_E = -1.4426950408889634


def _suffix_mask(n):
    j = lax.broadcasted_iota(jnp.int32, (2 * n, n), 0)
    s = lax.broadcasted_iota(jnp.int32, (2 * n, n), 1)
    return ((j >= s) & ((j < n) | (j >= s + n))).astype(BF16)


def _miss(z, mask):
    sp = jnp.maximum(z, 0.0) + jnp.log(1.0 + jnp.exp2(jnp.abs(z) * NEG_LOG2_E))
    return sp if mask is None else jnp.where(mask, sp, 0.0)


def _suffix_sums(miss, tri):
    hi = miss.astype(BF16)
    lo = (miss - hi.astype(F32)).astype(BF16)
    return jnp.dot(jnp.concatenate([hi, lo], axis=1), tri, preferred_element_type=F32)


def _stick_weights(z, misses_from_here, mask):
    w = jnp.exp(z - misses_from_here)
    return w if mask is None else jnp.where(mask, w, 0.0)


def _attn_prompt_kernel(bias_ref, q_ref, kt_ref, vt_ref, o_ref, acc_ref, later_ref):
    tb = ATT_BLOCK
    hp = pl.program_id(1)
    qi = pl.program_id(2)
    q = q_ref[0]
    lane = lax.broadcasted_iota(jnp.int32, (1, LANES), 1)
    tri = _suffix_mask(tb)
    diag = lax.broadcasted_iota(jnp.int32, (tb, tb), 1) < lax.broadcasted_iota(jnp.int32, (tb, tb), 0)
    heads = range(HEADS_PER_LANE_TILE)
    q_aug = []
    for hh in heads:
        qm = jnp.where((lane >= hh * HEAD_DIM) & (lane < (hh + 1) * HEAD_DIM), q, jnp.zeros_like(q))
        rest = jnp.full((1, LANES), bias_ref[hp * HEADS_PER_LANE_TILE + hh], F32)
        terms = jnp.zeros((1, LANES), F32)
        for t in range(BIAS_TERMS):
            term = rest.astype(BF16).astype(F32)
            terms = jnp.where(lane == t, term, terms)
            rest = rest - term
        q_aug.append(jnp.concatenate([qm, jnp.broadcast_to(terms, qm.shape).astype(BF16)], axis=1))
    ones_rows = (lax.broadcasted_iota(jnp.int32, (LANES, tb), 0) < BIAS_TERMS).astype(BF16)
    neg_tri = -tri
    acc_ref[...] = jnp.zeros_like(acc_ref)
    later_ref[...] = jnp.zeros_like(later_ref)

    def visit(blocks):
        chains = []
        for kj, subs_masks in blocks:
            kb = jnp.concatenate([kt_ref[0, 0, kj], ones_rows], axis=0)
            kb_minus_tri = jnp.concatenate([kb, neg_tri], axis=0)
            vb = vt_ref[0, 0, kj]
            chains += [(hh, slice(sub * tb, (sub + 1) * tb), mask, kb, kb_minus_tri, vb)
                       for hh in heads for sub, mask in subs_masks]
        zs = [jnp.dot(q_aug[hh][rows], kb, preferred_element_type=F32) for hh, rows, _, kb, _, _ in chains]
        hi_lo, z0 = [], []
        for z, chain in zip(zs, chains):
            miss = _miss(z, chain[2])
            hi = miss.astype(BF16)
            hi_lo.append(jnp.concatenate([hi, (miss - hi.astype(F32)).astype(BF16)], axis=1))
            z0.append(z[:, 0:1])
        log_ws = [jnp.dot(jnp.concatenate([q_aug[hh][rows], hl], axis=1), kmt, preferred_element_type=F32)
                  for hl, (hh, rows, _, _, kmt, _) in zip(hi_lo, chains)]
        for lw, z_first, (hh, rows, mask, _, _, vb) in zip(log_ws, z0, chains):
            c = later_ref[hh, rows]
            w = jnp.exp(lw - c)
            w = (w if mask is None else jnp.where(mask, w, 0.0)).astype(BF16)
            acc_ref[hh, rows] += lax.dot_general(w, vb, (((1,), (1,)), ((), ())), preferred_element_type=F32)
            later_ref[hh, rows] = c + (z_first - lw[:, 0:1])

    first = qi * Q_SUBS
    visit([(first + lead, [(lead, diag)] + [(sub, None) for sub in range(lead + 1, Q_SUBS)])
           for lead in reversed(range(Q_SUBS))])

    all_subs = [(sub, None) for sub in range(Q_SUBS)]

    def body(t, carry):
        kj = first - 1 - LOOP_BLOCKS * t
        visit([(kj - i, all_subs) for i in range(LOOP_BLOCKS)])
        return carry

    lax.fori_loop(0, first // LOOP_BLOCKS, body, 0)
    o_ref[0] = jnp.where(lane < HEAD_DIM, acc_ref[0], acc_ref[1]).astype(o_ref.dtype)


def _attn_prompt(q, kt, vt, bias):
    b, s, hd = q.shape
    n_kb = kt.shape[2]
    tq = Q_SUBS * ATT_BLOCK
    qspec = pl.BlockSpec((1, tq, LANES), lambda i, h, j: (i, j, h))
    kspec = pl.BlockSpec((1, 1, n_kb, LANES, ATT_BLOCK), lambda i, h, j: (i, h, 0, 0, 0))
    return pl.pallas_call(
        _attn_prompt_kernel,
        grid=(b, hd // LANES, s // tq),
        in_specs=[pl.BlockSpec(memory_space=pltpu.SMEM), qspec, kspec, kspec],
        out_specs=qspec,
        out_shape=jax.ShapeDtypeStruct((b, s, hd), BF16),
        scratch_shapes=[pltpu.VMEM((HEADS_PER_LANE_TILE, tq, LANES), F32),
                        pltpu.VMEM((HEADS_PER_LANE_TILE, tq, 1), F32)],
        compiler_params=_params(("parallel", "parallel", "arbitrary")),
        name="attn_prompt",
    )(bias, q, kt, vt)


def _attn_paged_kernel(pt_ref, q_ref, knew_ref, vnew_ref, *rest, pages_per_step):
    del pt_ref
    kpage_refs = rest[:pages_per_step]
    vpage_refs = rest[pages_per_step:2 * pages_per_step]
    bias_ref, o_ref, qbd_ref, acc_ref, later_ref = rest[2 * pages_per_step:]
    j = pl.program_id(1)
    nq = q_ref.shape[0]
    hd = q_ref.shape[1]
    n_heads = hd // HEAD_DIM
    rows = n_heads * nq
    tk = knew_ref.shape[1]
    row = lax.broadcasted_iota(jnp.int32, (rows, 1), 0)
    tri = _suffix_mask(tk)
    contract_last = (((1,), (1,)), ((), ()))

    @pl.when(j == 0)
    def _():
        lane = lax.broadcasted_iota(jnp.int32, (1, hd), 1)
        q_rep = jnp.concatenate([q_ref[...]] * n_heads, axis=0)
        qbd = jnp.where(lane // HEAD_DIM == row // nq, q_rep, 0.0).astype(BF16)
        qbd_ref[...] = qbd
        col = lax.broadcasted_iota(jnp.int32, (1, tk), 1)
        mask = col < row % nq
        z = lax.dot_general(qbd, knew_ref[0].astype(BF16), contract_last, preferred_element_type=F32) + bias_ref[...]
        sums = _suffix_sums(_miss(z, mask), tri)
        w = _stick_weights(z, sums, mask).astype(BF16)
        acc_ref[...] = jnp.dot(w, vnew_ref[0].astype(BF16), preferred_element_type=F32)
        later_ref[...] = sums[:, 0:1]

    @pl.when(j > 0)
    def _():
        order = list(reversed(range(pages_per_step)))
        gr, gf = rows // PAGED_HEAD_GROUPS, hd // PAGED_HEAD_GROUPS
        groups = [(slice(g * gr, (g + 1) * gr), slice(g * gf, (g + 1) * gf)) for g in range(PAGED_HEAD_GROUPS)]
        zs = []
        for c in order:
            kt = kpage_refs[c][0].reshape(hd, tk).astype(BF16)
            zs.append(jnp.concatenate([jnp.dot(qbd_ref[r, f], kt[f], preferred_element_type=F32)
                                       for r, f in groups], axis=0) + bias_ref[...])
        sums = [_suffix_sums(_miss(z, None), tri) for z in zs]
        later_blocks = later_ref[...]
        ws = []
        for z, sm in zip(zs, sums):
            ws.append(_stick_weights(z, sm + later_blocks, None).astype(BF16))
            later_blocks = later_blocks + sm[:, 0:1]
        later_ref[...] = later_blocks
        w_all = jnp.concatenate(ws, axis=1)
        v_all = jnp.concatenate([vpage_refs[c][0].reshape(hd, tk).astype(BF16) for c in order], axis=1)
        for r, f in groups:
            acc_ref[r, f] += lax.dot_general(w_all[r], v_all[f], contract_last, preferred_element_type=F32)

    @pl.when(j == pl.num_programs(1) - 1)
    def _():
        lane = lax.broadcasted_iota(jnp.int32, (1, hd), 1)
        out = jnp.zeros((nq, hd), F32)
        for h in range(n_heads):
            out = out + jnp.where(lane // HEAD_DIM == h, acc_ref[h * nq:(h + 1) * nq, :], 0.0)
        o_ref[...] = out


def _attn_paged(q, k_new, v_new, cache_kt, cache_vt, page_table, bias, n_new):
    n_seq, n_pages = page_table.shape
    hd = q.shape[1]
    n_heads = hd // HEAD_DIM
    rows = n_heads * n_new
    page = cache_kt.shape[3]
    pps = PAGES_PER_STEP
    seq_rows = pl.BlockSpec((n_new, hd), lambda n, j, pt: (n, 0))
    new_spec = pl.BlockSpec((1, page, hd), lambda n, j, pt: (n, 0, 0))

    def page_spec(c):
        return pl.BlockSpec((1, n_heads, HEAD_DIM, page),
                            lambda n, j, pt: (pt[n, n_pages - jnp.maximum(j, 1) * pps + c], 0, 0, 0))

    grid_spec = pltpu.PrefetchScalarGridSpec(
        num_scalar_prefetch=1,
        grid=(n_seq, n_pages // pps + 1),
        in_specs=([seq_rows, new_spec, new_spec] + [page_spec(c) for c in range(pps)] * 2
                  + [pl.BlockSpec((rows, 1), lambda n, j, pt: (0, 0))]),
        out_specs=seq_rows,
        scratch_shapes=[pltpu.VMEM((rows, hd), BF16), pltpu.VMEM((rows, hd), F32), pltpu.VMEM((rows, 1), F32)],
    )
    return pl.pallas_call(
        functools.partial(_attn_paged_kernel, pages_per_step=pps),
        grid_spec=grid_spec,
        out_shape=jax.ShapeDtypeStruct((n_seq * n_new, hd), F32),
        compiler_params=_params(("parallel", "arbitrary")),
        name="attn_paged",
    )(page_table, q, k_new, v_new, *([cache_kt] * pps), *([cache_vt] * pps),
      jnp.repeat(bias, n_new).reshape(rows, 1))


def _s5_layer(xp, xs, n_prompt, h0_re, h0_im, prm):
    (g_pre, g_post, w_in, log_dt, a_re, a_im, b_re, b_im, c_re, c_im, d_skip, w_glu, b_glu, w_out) = prm
    d = xp.shape[1]
    n_groups, n_state = a_re.shape
    gc = b_re.shape[-1]
    width = n_groups * gc
    w1, w2, coef = _ssm_prep(log_dt, a_re, a_im, b_re, b_im, c_re, c_im)
    w_in_b, w_glu_b, w_out_b = w_in.astype(BF16), w_glu.astype(BF16), w_out.astype(BF16)
    uz_outs = [(0, width, 1.0, F32), (width, 2 * width, 1.0, BF16)]

    u, z = _norm_matmul(xp, g_pre, w_in_b, uz_outs, 512)
    y, hfin = _ssm_seq(u, w1, w2, coef, n_prompt)
    xp = _s5_post(y, u, z, xp, d_skip, w_glu_b, b_glu, w_out_b, g_post, 512)

    n_seq = h0_re.shape[0]
    t_new = xs.shape[0] // n_seq
    us, zs = _norm_matmul(xs, g_pre, w_in_b, uz_outs, 512)
    u8 = (us.reshape(n_seq, t_new, n_groups, gc).transpose(2, 0, 1, 3)
          .reshape(n_groups, n_seq, t_new * gc).astype(BF16))
    h0 = jnp.concatenate([h0_re, h0_im], axis=-1).transpose(1, 0, 2)
    h0s = jnp.concatenate([h0_im, h0_re], axis=-1).transpose(1, 0, 2)
    y8, hf8 = _ssm_step(u8, h0, h0s, w1, w2, coef)
    ys = y8.reshape(n_groups, n_seq, t_new, gc).transpose(1, 2, 0, 3).reshape(xs.shape[0], width)
    xs = _s5_post(ys, us, zs, xs, d_skip, w_glu_b, b_glu, w_out_b, g_post, 512)
    hf8 = hf8.transpose(1, 0, 2)
    return (xp, xs, hfin[..., :n_state], hfin[..., n_state:], hf8[..., :n_state], hf8[..., n_state:])


def kernel(x_prompt, x_sample, state_ssm_re, state_ssm_im, cache_k, cache_v, page_table, a_norm_pre, a_norm_post, a_w_in, a_log_dt, a_A_re, a_A_im, a_B_re, a_B_im, a_C_re, a_C_im, a_D, a_w_glu, a_b_glu, a_w_out, kv_norm, w_kv, b_norm_pre, b_norm_post, b_w_in, b_logit_bias, b_w_out):
    n_prompt, seq, d = x_prompt.shape
    n_seq, t_new, _ = x_sample.shape
    n_a = a_w_in.shape[0]
    n_b = b_w_in.shape[0]
    att = w_kv.shape[1] // 2
    n_heads = att // HEAD_DIM
    assert t_new * 2 == SSM_CHUNK and cache_k.shape[1] == PAGE_SIZE and HEAD_DIM ** -0.5 == 0.125

    xp = x_prompt.reshape(n_prompt * seq, d)
    xs = x_sample.reshape(n_seq * t_new, d)
    p_re, p_im, s_re, s_im = [], [], [], []
    for i in range(n_a):
        prm = (a_norm_pre[i], a_norm_post[i], a_w_in[i], a_log_dt[i], a_A_re[i], a_A_im[i], a_B_re[i], a_B_im[i],
               a_C_re[i], a_C_im[i], a_D[i], a_w_glu[i], a_b_glu[i], a_w_out[i])
        xp, xs, hr, hi, sr, si = _s5_layer(xp, xs, n_prompt, state_ssm_re[i], state_ssm_im[i], prm)
        p_re.append(hr)
        p_im.append(hi)
        s_re.append(sr)
        s_im.append(si)

    w_kv_b = w_kv.astype(BF16)
    kt_p, vt_p, kt_pb, vt_pb = _kv_proj(xp, kv_norm, w_kv_b.T, n_prompt, 512)
    k_p = kt_p.reshape(n_prompt, n_heads, HEAD_DIM, seq).transpose(0, 3, 1, 2)
    v_p = vt_p.reshape(n_prompt, n_heads, HEAD_DIM, seq).transpose(0, 3, 1, 2)
    k_s, v_s = _norm_matmul(xs, kv_norm, w_kv_b, [(0, att, 1.0, F32), (att, 2 * att, 1.0, F32)], 512)
    pad_new = lambda a: jnp.pad(a.reshape(n_seq, t_new, att), ((0, 0), (0, PAGE_SIZE - t_new), (0, 0)))
    k_s_pad, v_s_pad = pad_new(k_s), pad_new(v_s)
    cache_kt = cache_k.transpose(0, 2, 3, 1)
    cache_vt = cache_v.transpose(0, 2, 3, 1)

    scale = HEAD_DIM ** -0.5
    for j in range(n_b):
        w_in_b, w_out_b = b_w_in[j].astype(BF16), b_w_out[j].astype(BF16)
        q, g = _norm_matmul(xp, b_norm_pre[j], w_in_b, [(0, att, scale, BF16), (att, 2 * att, 1.0, BF16)], 512)
        o = _attn_prompt(q.reshape(n_prompt, seq, att), kt_pb, vt_pb, b_logit_bias[j])
        xp = _sb_post(o.reshape(n_prompt * seq, att), g, xp, w_out_b, b_norm_post[j], 512)
        qs, gs = _norm_matmul(xs, b_norm_pre[j], w_in_b, [(0, att, scale, F32), (att, 2 * att, 1.0, F32)], 512)
        os_ = _attn_paged(qs, k_s_pad, v_s_pad, cache_kt, cache_vt, page_table, b_logit_bias[j], t_new)
        xs = _sb_post(os_, gs, xs, w_out_b, b_norm_post[j], 512)

    return (xp.reshape(n_prompt, seq, d), xs.reshape(n_seq, t_new, d),
            jnp.stack(p_re), jnp.stack(p_im),
            k_p, v_p,
            jnp.stack(s_re), jnp.stack(s_im),
            k_s.reshape(n_seq, t_new, n_heads, HEAD_DIM), v_s.reshape(n_seq, t_new, n_heads, HEAD_DIM))
```
